```python
import jax, jax.numpy as jnp
from jax import lax
import numpy as np

D_MODEL = 1024
BATCH = 16
SEQ = 2048
DEPTH = 2
DEC_BATCH = 1
DEC_SEQ = 16384
PAST_LEN = 128

MIX_POOL = D_MODEL // 4
MIX_LRU = D_MODEL // 2
MIX_CONV = D_MODEL // 4
MIX_WIDTH = MIX_POOL + MIX_LRU + MIX_CONV
IN_COLS = MIX_POOL + 2 * MIX_LRU + 2 * MIX_CONV
POOL_WINDOWS = (2, 4, 8, 16)
POOL_GROUPS = len(POOL_WINDOWS)
POOL_GROUP_WIDTH = MIX_POOL // POOL_GROUPS
LRU_HEADS = 8
LRU_HEAD_DIM = MIX_LRU // LRU_HEADS
LRU_CONV_WIDTH = 4
RG_C = 8.0
CONV_WIDTH = 31
CONV_GROUPS = 4
N_EXPERTS = 16
CAPACITY_FACTOR = 2
D_FF_EXPERT = 11 * D_MODEL // 8
RMS_EPS = 1e-6
GN_EPS = 1e-5

kernel_name = 'hybrid_pool_rglru_conformer_ec_encoder'


def rmsnorm(x, g):
    x32 = x.astype(jnp.float32)
    y = x32 * lax.rsqrt(jnp.mean(x32 * x32, axis=-1, keepdims=True) + RMS_EPS)
    return (y * g.astype(jnp.float32)).astype(x.dtype)


def group_norm(x, g, b, groups):
    B, S, C = x.shape
    x32 = x.astype(jnp.float32).reshape(B, S, groups, C // groups)
    mu = jnp.mean(x32, axis=-1, keepdims=True)
    var = jnp.mean(jnp.square(x32 - mu), axis=-1, keepdims=True)
    y = ((x32 - mu) * lax.rsqrt(var + GN_EPS)).reshape(B, S, C)
    return (y * g.astype(jnp.float32) + b.astype(jnp.float32)).astype(x.dtype)


def depthwise_conv(x, w, b, pad_lo, pad_hi):
    k, c = w.shape
    y = lax.conv_general_dilated(x, w.reshape(k, 1, c).astype(x.dtype), window_strides=(1,),
                                 padding=[(pad_lo, pad_hi)], dimension_numbers=('NWC', 'WIO', 'NWC'),
                                 feature_group_count=c)
    return y + b.astype(x.dtype)


def pool_mixer(u, w, scale):
    B, S, C = u.shape
    u32 = u.astype(jnp.float32)
    cs = jnp.concatenate([jnp.zeros((B, 1, C), jnp.float32), jnp.cumsum(u32, axis=1)], axis=1)
    pos = jnp.arange(S)
    outs = []
    for g, win in enumerate(POOL_WINDOWS):
        lo_c, hi_c = g * POOL_GROUP_WIDTH, (g + 1) * POOL_GROUP_WIDTH
        lo = jnp.clip(pos - win // 2, 0, S)
        hi = jnp.clip(pos + win // 2, 0, S)
        csg = cs[:, :, lo_c:hi_c]
        mean = (csg[:, hi] - csg[:, lo]) / (hi - lo).astype(jnp.float32)[None, :, None]
        outs.append(mean - u32[:, :, lo_c:hi_c])
    d = jnp.stack(outs, axis=2).astype(u.dtype)
    y = jnp.einsum('bsgi,gij->bsgj', d, w).reshape(B, S, C)
    return y * scale


def _linear_combine(left, right):
    a1, b1 = left
    a2, b2 = right
    return a1 * a2, a2 * b1 + b2


def rg_lru(u, wa, ba, wx, bx, lam, reverse):
    B, S, C = u.shape
    uh = u.reshape(B, S, LRU_HEADS, LRU_HEAD_DIM)
    r = jax.nn.sigmoid(jnp.einsum('bshi,hij->bshj', uh, wa).reshape(B, S, C) + ba)
    i = jax.nn.sigmoid(jnp.einsum('bshi,hij->bshj', uh, wx).reshape(B, S, C) + bx)
    log_a = -RG_C * r.astype(jnp.float32) * jax.nn.softplus(-lam.astype(jnp.float32))
    a = jnp.exp(log_a)
    b = jnp.sqrt(-jnp.expm1(2.0 * log_a)) * (i * u).astype(jnp.float32)
    _, h = lax.associative_scan(_linear_combine, (a, b), reverse=reverse, axis=1)
    return h.astype(u.dtype)


def conformer_conv(u, dw_w, dw_b, gn_g, gn_b, pw_w, pw_b):
    a, g = jnp.split(u, 2, axis=-1)
    v = a * jax.nn.sigmoid(g)
    v = depthwise_conv(v, dw_w, dw_b, CONV_WIDTH // 2, CONV_WIDTH // 2)
    v = jax.nn.silu(group_norm(v, gn_g, gn_b, CONV_GROUPS))
    return v @ pw_w + pw_b


def expert_choice_ffn(x, router_w, wg, wu, wd):
    B, S, D = x.shape
    n = B * S
    xt = x.reshape(n, D)
    aff = jax.nn.softmax((xt @ router_w).astype(jnp.float32), axis=-1)
    cap = max(1, CAPACITY_FACTOR * n // N_EXPERTS)
    gate, idx = lax.top_k(aff.T, cap)
    xe = xt[idx]
    h = jax.nn.silu(jnp.einsum('ecd,edf->ecf', xe, wg)) * jnp.einsum('ecd,edf->ecf', xe, wu)
    ye = jnp.einsum('ecf,efd->ecd', h, wd) * gate[..., None].astype(x.dtype)
    out = jnp.zeros((n, D), x.dtype).at[idx.reshape(-1)].add(ye.reshape(-1, D))
    return out.reshape(B, S, D)


def encoder_trunk(x, norm1_g, w_in, pool_w, pool_scale, lru_conv_w, lru_conv_b, lru_wa, lru_ba,
                  lru_wx, lru_bx, lru_lambda, conv_dw_w, conv_dw_b, conv_gn_g, conv_gn_b, conv_pw_w,
                  conv_pw_b, w_out, norm2_g, router_w, exp_w_gate, exp_w_up, exp_w_down, final_g):
    o1 = MIX_POOL
    o2 = o1 + MIX_LRU
    o3 = o2 + MIX_LRU
    for l in range(DEPTH):
        h = rmsnorm(x, norm1_g[l])
        z = h @ w_in[l]
        u_pool, u_lru, u_gate, u_conv = z[..., :o1], z[..., o1:o2], z[..., o2:o3], z[..., o3:]
        y_pool = pool_mixer(u_pool, pool_w[l], pool_scale[l])
        c = depthwise_conv(u_lru, lru_conv_w[l], lru_conv_b[l], LRU_CONV_WIDTH // 2, LRU_CONV_WIDTH // 2 - 1)
        y_fwd = rg_lru(c, lru_wa[l, 0], lru_ba[l, 0], lru_wx[l, 0], lru_bx[l, 0], lru_lambda[l, 0], False)
        y_bwd = rg_lru(c, lru_wa[l, 1], lru_ba[l, 1], lru_wx[l, 1], lru_bx[l, 1], lru_lambda[l, 1], True)
        y_lru = (y_fwd + y_bwd) * jax.nn.gelu(u_gate)
        y_conv = conformer_conv(u_conv, conv_dw_w[l], conv_dw_b[l], conv_gn_g[l], conv_gn_b[l],
                                conv_pw_w[l], conv_pw_b[l])
        x = x + jnp.concatenate([y_pool, y_lru, y_conv], axis=-1) @ w_out[l]
        x = x + expert_choice_ffn(rmsnorm(x, norm2_g[l]), router_w[l], exp_w_gate[l], exp_w_up[l], exp_w_down[l])
    return rmsnorm(x, final_g)


def setup_inputs(seed: int = 0) -> dict:
    key = jax.random.key(seed)
    ks = jax.random.split(key, 32)

    def nrm(k, shape, scale):
        return jax.random.normal(k, shape, jnp.float32) * scale

    a0 = jax.random.uniform(ks[13], (DEPTH, 2, MIX_LRU), jnp.float32, minval=0.9, maxval=0.999)
    lru_lambda = -jnp.log(jnp.expm1(-jnp.log(a0) / RG_C))
    return {
        'x_prompt': nrm(ks[0], (BATCH, SEQ, D_MODEL), 1.0),
        'x_sample': nrm(ks[1], (DEC_BATCH, DEC_SEQ, D_MODEL), 1.0),
        'norm1_g': 1.0 + nrm(ks[2], (DEPTH, D_MODEL), 0.05),
        'w_in': nrm(ks[3], (DEPTH, D_MODEL, IN_COLS), D_MODEL ** -0.5),
        'pool_w': nrm(ks[4], (DEPTH, POOL_GROUPS, POOL_GROUP_WIDTH, POOL_GROUP_WIDTH), POOL_GROUP_WIDTH ** -0.5),
        'pool_scale': 1.0 + nrm(ks[5], (DEPTH, MIX_POOL), 0.1),
        'lru_conv_w': nrm(ks[6], (DEPTH, LRU_CONV_WIDTH, MIX_LRU), LRU_CONV_WIDTH ** -0.5),
        'lru_conv_b': nrm(ks[7], (DEPTH, MIX_LRU), 0.02),
        'lru_wa': nrm(ks[8], (DEPTH, 2, LRU_HEADS, LRU_HEAD_DIM, LRU_HEAD_DIM), LRU_HEAD_DIM ** -0.5),
        'lru_ba': nrm(ks[9], (DEPTH, 2, MIX_LRU), 0.1),
        'lru_wx': nrm(ks[10], (DEPTH, 2, LRU_HEADS, LRU_HEAD_DIM, LRU_HEAD_DIM), LRU_HEAD_DIM ** -0.5),
        'lru_bx': nrm(ks[11], (DEPTH, 2, MIX_LRU), 0.1),
        'lru_lambda': lru_lambda,
        'conv_dw_w': nrm(ks[14], (DEPTH, CONV_WIDTH, MIX_CONV), CONV_WIDTH ** -0.5),
        'conv_dw_b': nrm(ks[15], (DEPTH, MIX_CONV), 0.02),
        'conv_gn_g': 1.0 + nrm(ks[16], (DEPTH, MIX_CONV), 0.05),
        'conv_gn_b': nrm(ks[17], (DEPTH, MIX_CONV), 0.02),
        'conv_pw_w': nrm(ks[18], (DEPTH, MIX_CONV, MIX_CONV), MIX_CONV ** -0.5),
        'conv_pw_b': nrm(ks[19], (DEPTH, MIX_CONV), 0.02),
        'w_out': nrm(ks[20], (DEPTH, MIX_WIDTH, D_MODEL), MIX_WIDTH ** -0.5),
        'norm2_g': 1.0 + nrm(ks[21], (DEPTH, D_MODEL), 0.05),
        'router_w': nrm(ks[22], (DEPTH, D_MODEL, N_EXPERTS), D_MODEL ** -0.5),
        'exp_w_gate': nrm(ks[23], (DEPTH, N_EXPERTS, D_MODEL, D_FF_EXPERT), D_MODEL ** -0.5),
        'exp_w_up': nrm(ks[24], (DEPTH, N_EXPERTS, D_MODEL, D_FF_EXPERT), D_MODEL ** -0.5),
        'exp_w_down': nrm(ks[25], (DEPTH, N_EXPERTS, D_FF_EXPERT, D_MODEL), D_FF_EXPERT ** -0.5),
        'final_g': 1.0 + nrm(ks[26], (D_MODEL,), 0.05),
    }


def reference(x_prompt, x_sample, norm1_g, w_in, pool_w, pool_scale, lru_conv_w, lru_conv_b, lru_wa,
              lru_ba, lru_wx, lru_bx, lru_lambda, conv_dw_w, conv_dw_b, conv_gn_g, conv_gn_b, conv_pw_w,
              conv_pw_b, w_out, norm2_g, router_w, exp_w_gate, exp_w_up, exp_w_down, final_g):
    weights = (norm1_g, w_in, pool_w, pool_scale, lru_conv_w, lru_conv_b, lru_wa, lru_ba, lru_wx, lru_bx,
               lru_lambda, conv_dw_w, conv_dw_b, conv_gn_g, conv_gn_b, conv_pw_w, conv_pw_b, w_out,
               norm2_g, router_w, exp_w_gate, exp_w_up, exp_w_down, final_g)
    y_prompt = encoder_trunk(x_prompt, *weights)
    y_sample = encoder_trunk(x_sample, *weights)
    return (y_prompt, y_sample)
```

```python
import functools

import jax
import jax.numpy as jnp
from jax import lax
from jax.experimental import pallas as pl
from jax.experimental.pallas import tpu as pltpu

F32 = jnp.float32
BF16 = jnp.bfloat16

D_MODEL = 1024
MIX_POOL = 256
MIX_LRU = 512
MIX_CONV = 256
IN_COLS = MIX_POOL + 2 * MIX_LRU + 2 * MIX_CONV
POOL_GROUP_WIDTH = 64
POOL_MAX_HALF = 8
LRU_CONV_WIDTH = 4
RG_C = 8.0
CONV_WIDTH = 31
CONV_GROUPS = 4
N_EXPERTS = 16
CAPACITY_FACTOR = 2
D_FF = 11 * D_MODEL // 8
RMS_EPS = 1e-6
GN_EPS = 1e-5

SUBLANES = 8
ROW_TILE = 512
SCAN_TILE = 512
HALO = 16
TOK_BLOCK = 256
SLOT_CHUNK = 512
FFN_ROWS = 512
VMEM_LIMIT = 48 * 1024 * 1024


def _cparams(*sem):
    return pltpu.CompilerParams(dimension_semantics=sem, vmem_limit_bytes=VMEM_LIMIT)


def _rms(x, g):
    return x * lax.rsqrt(jnp.mean(x * x, axis=-1, keepdims=True) + RMS_EPS) * g


def _split_bf16(x):
    hi = x.astype(BF16)
    lo = (x - hi.astype(F32)).astype(BF16)
    return hi, lo


def _dot(a, b):
    return jnp.dot(a, b, preferred_element_type=F32)


def _in_proj_kernel(x_ref, g_ref, w_ref, zp_ref, zl_ref, zg_ref, zc_ref):
    h = _rms(x_ref[...], g_ref[...])
    z = _dot(h.astype(BF16), w_ref[...])
    o1 = MIX_POOL
    o2 = o1 + MIX_LRU
    o3 = o2 + MIX_LRU
    zp_ref[...] = z[:, :o1]
    zl_ref[...] = z[:, o1:o2]
    zg_ref[...] = z[:, o2:o3]
    zc_ref[...] = z[:, o3:]


def _in_proj(x, g, w_bf16):
    n = x.shape[0]
    tm = min(ROW_TILE, n)
    row = lambda i: (i, 0)
    full = lambda i: (0, 0)
    widths = (MIX_POOL, MIX_LRU, MIX_LRU, 2 * MIX_CONV)
    return pl.pallas_call(
        _in_proj_kernel,
        grid=(n // tm,),
        in_specs=[pl.BlockSpec((tm, D_MODEL), row),
                  pl.BlockSpec((1, D_MODEL), full),
                  pl.BlockSpec((D_MODEL, IN_COLS), full)],
        out_specs=[pl.BlockSpec((tm, w), row) for w in widths],
        out_shape=[jax.ShapeDtypeStruct((n, w), F32) for w in widths],
        compiler_params=_cparams("parallel"),
        name="in_proj",
    )(x, g, w_bf16)


def _lru_scan_kernel(u_ref, prev_ref, next_ref, cw_ref, cb_ref, wa_ref, ba_ref, wx_ref, bx_ref,
                     lam_ref, y_ref, ext_ref, a_ref, b_ref, h_ref, *, tiles_per_seq, reverse):
    t = u_ref.shape[0]
    j = pl.program_id(1)
    jj = tiles_per_seq - 1 - j if reverse else j

    @pl.when(j == 0)
    def _():
        h_ref[...] = jnp.zeros_like(h_ref)

    ext_ref[0:SUBLANES, :] = jnp.where(jj > 0, prev_ref[...], 0.0)
    ext_ref[SUBLANES:SUBLANES + t, :] = u_ref[...]
    ext_ref[SUBLANES + t:2 * SUBLANES + t, :] = jnp.where(jj < tiles_per_seq - 1, next_ref[...], 0.0)

    pad_lo = LRU_CONV_WIDTH // 2
    c = cb_ref[...]
    for k in range(LRU_CONV_WIDTH):
        off = SUBLANES - pad_lo + k
        c = c + cw_ref[k:k + 1, :] * ext_ref[off:off + t, :]
    cb16 = c.astype(BF16)
    r = jax.nn.sigmoid(_dot(cb16, wa_ref[...]) + ba_ref[...])
    i = jax.nn.sigmoid(_dot(cb16, wx_ref[...]) + bx_ref[...])
    neg_lam = -lam_ref[...]
    softplus = jnp.maximum(neg_lam, 0.0) + jnp.log1p(jnp.exp(-jnp.abs(neg_lam)))
    log_a = -RG_C * r * softplus
    th = jnp.tanh(log_a)
    one_minus_a2 = -2.0 * th / (1.0 - th)
    a_ref[...] = jnp.exp(log_a)
    b_ref[...] = jnp.sqrt(one_minus_a2) * (i * c)

    groups = t // SUBLANES
    sub = lax.broadcasted_iota(jnp.int32, (SUBLANES, MIX_LRU), 0)

    def body(g, carry):
        gi = groups - 1 - g if reverse else g
        start = pl.multiple_of(gi * SUBLANES, SUBLANES)
        a8 = a_ref[pl.ds(start, SUBLANES), :]
        b8 = b_ref[pl.ds(start, SUBLANES), :]
        for k in (1, 2, 4):
            shift = SUBLANES - k if reverse else k
            m = (sub < SUBLANES - k) if reverse else (sub >= k)
            a_sh = pltpu.roll(a8, shift, axis=0)
            b_sh = pltpu.roll(b8, shift, axis=0)
            b8 = jnp.where(m, a8 * b_sh + b8, b8)
            a8 = jnp.where(m, a8 * a_sh, a8)
        h = a8 * carry + b8
        y_ref[pl.ds(start, SUBLANES), :] = h
        edge = h[0:1, :] if reverse else h[SUBLANES - 1:SUBLANES, :]
        return jnp.broadcast_to(edge, (SUBLANES, MIX_LRU))

    h_ref[...] = lax.fori_loop(0, groups, body, h_ref[...], unroll=4)


def _lru_scan(z_lru, cw, cb, wa_bd, ba, wx_bd, bx, lam, *, seq_len, reverse):
    n = z_lru.shape[0]
    t = min(SCAN_TILE, seq_len)
    nt = seq_len // t
    nseq = n // seq_len
    halo_blocks = n // SUBLANES
    tile_blocks = t // SUBLANES

    def tile_of(s, j):
        return s * nt + ((nt - 1 - j) if reverse else j)

    def main_map(s, j):
        return (tile_of(s, j), 0)

    def prev_map(s, j):
        return (jnp.maximum(tile_of(s, j) * tile_blocks - 1, 0), 0)

    def next_map(s, j):
        return (jnp.minimum((tile_of(s, j) + 1) * tile_blocks, halo_blocks - 1), 0)

    full = lambda s, j: (0, 0)
    vec = pl.BlockSpec((1, MIX_LRU), full)
    mat = pl.BlockSpec((MIX_LRU, MIX_LRU), full)
    kern = functools.partial(_lru_scan_kernel, tiles_per_seq=nt, reverse=reverse)
    return pl.pallas_call(
        kern,
        grid=(nseq, nt),
        in_specs=[pl.BlockSpec((t, MIX_LRU), main_map),
                  pl.BlockSpec((SUBLANES, MIX_LRU), prev_map),
                  pl.BlockSpec((SUBLANES, MIX_LRU), next_map),
                  pl.BlockSpec((LRU_CONV_WIDTH, MIX_LRU), full), vec, mat, vec, mat, vec, vec],
        out_specs=pl.BlockSpec((t, MIX_LRU), main_map),
        out_shape=jax.ShapeDtypeStruct((n, MIX_LRU), F32),
        scratch_shapes=[pltpu.VMEM((t + 2 * SUBLANES, MIX_LRU), F32),
                        pltpu.VMEM((t, MIX_LRU), F32),
                        pltpu.VMEM((t, MIX_LRU), F32),
                        pltpu.VMEM((SUBLANES, MIX_LRU), F32)],
        compiler_params=_cparams("arbitrary", "arbitrary"),
        name="lru_scan_bwd" if reverse else "lru_scan_fwd",
    )(z_lru, z_lru, z_lru, cw, cb, wa_bd, ba, wx_bd, bx, lam)


def _gelu_tanh(x):
    return 0.5 * x * (1.0 + jnp.tanh(0.7978845608028654 * (x + 0.044715 * (x * x * x))))


def _mix_out_kernel(x_ref, zp_ref, zp_prev_ref, zp_next_ref, zc_ref, zc_prev_ref, zc_next_ref,
                    zg_ref, yf_ref, yb_ref,
                    pool_w_ref, pool_s_ref, dw_w_ref, dw_b_ref, gn_g_ref, gn_b_ref, gavg_ref,
                    pw_w_ref, pw_b_ref, wout_ref, g2_ref, rw_hi_ref, rw_lo_ref,
                    x1_ref, xn_ref, aff_ref,
                    pext_ref, vext_ref, *, tiles_per_seq, seq_len):
    t = x_ref.shape[0]
    jj = pl.program_id(0) % tiles_per_seq
    has_prev = jj > 0
    has_next = jj < tiles_per_seq - 1

    pext_ref[0:HALO, :] = jnp.where(has_prev, zp_prev_ref[...], 0.0)
    pext_ref[HALO:HALO + t, :] = zp_ref[...]
    pext_ref[HALO + t:2 * HALO + t, :] = jnp.where(has_next, zp_next_ref[...], 0.0)
    lane = lax.broadcasted_iota(jnp.int32, (1, MIX_POOL), 1)
    half = jnp.left_shift(1, lane // POOL_GROUP_WIDTH)
    acc = jnp.zeros((t, MIX_POOL), F32)
    for d in range(-POOL_MAX_HALF, POOL_MAX_HALF):
        in_window = jnp.where((d >= -half) & (d < half), 1.0, 0.0)
        acc = acc + in_window * pext_ref[HALO + d:HALO + d + t, :]
    pos = jj * t + lax.broadcasted_iota(jnp.int32, (t, MIX_POOL), 0)
    count = jnp.minimum(pos + half, seq_len) - jnp.maximum(pos - half, 0)
    dmean = acc / count.astype(F32) - zp_ref[...]
    y_pool = _dot(dmean.astype(BF16), pool_w_ref[...]) * pool_s_ref[...]

    def glu(z):
        return z[:, :MIX_CONV] * jax.nn.sigmoid(z[:, MIX_CONV:])

    vext_ref[0:HALO, :] = glu(jnp.where(has_prev, zc_prev_ref[...], 0.0))
    vext_ref[HALO:HALO + t, :] = glu(zc_ref[...])
    vext_ref[HALO + t:2 * HALO + t, :] = glu(jnp.where(has_next, zc_next_ref[...], 0.0))
    v = jnp.zeros((t, MIX_CONV), F32) + dw_b_ref[...]
    for k in range(CONV_WIDTH):
        off = HALO - CONV_WIDTH // 2 + k
        v = v + dw_w_ref[k:k + 1, :] * vext_ref[off:off + t, :]
    gavg = gavg_ref[...]
    v_hi, v_lo = _split_bf16(v)
    mu = _dot(v_hi, gavg) + _dot(v_lo, gavg)
    dv = v - mu
    sq_hi, sq_lo = _split_bf16(dv * dv)
    var = _dot(sq_hi, gavg) + _dot(sq_lo, gavg)
    yn = dv * lax.rsqrt(var + GN_EPS) * gn_g_ref[...] + gn_b_ref[...]
    sw = yn * jax.nn.sigmoid(yn)
    y_conv = _dot(sw.astype(BF16), pw_w_ref[...]) + pw_b_ref[...]

    y_lru = (yf_ref[...] + yb_ref[...]) * _gelu_tanh(zg_ref[...])

    o1 = MIX_POOL
    o2 = o1 + MIX_LRU
    mixed = (_dot(y_pool.astype(BF16), wout_ref[0:o1, :])
             + _dot(y_lru.astype(BF16), wout_ref[o1:o2, :])
             + _dot(y_conv.astype(BF16), wout_ref[o2:, :]))
    x1 = x_ref[...] + mixed
    x1_ref[...] = x1

    xn = _rms(x1, g2_ref[...])
    xn_ref[...] = xn.astype(BF16)
    xn_hi, xn_lo = _split_bf16(xn)
    nt_dims = (((1,), (1,)), ((), ()))
    logits = (lax.dot_general(rw_hi_ref[...], xn_hi, nt_dims, preferred_element_type=F32)
              + lax.dot_general(rw_hi_ref[...], xn_lo, nt_dims, preferred_element_type=F32)
              + lax.dot_general(rw_lo_ref[...], xn_hi, nt_dims, preferred_element_type=F32))
    e = jnp.exp(logits - jnp.max(logits, axis=0, keepdims=True))
    aff_ref[...] = e / jnp.sum(e, axis=0, keepdims=True)


def _mix_out(x, zp, zc, zg, yf, yb, wts, *, seq_len):
    n = x.shape[0]
    t = min(ROW_TILE, seq_len)
    nt = seq_len // t
    halo_blocks = n // HALO
    tile_blocks = t // HALO
    row = lambda i: (i, 0)
    prev = lambda i: (jnp.maximum(i * tile_blocks - 1, 0), 0)
    nxt = lambda i: (jnp.minimum((i + 1) * tile_blocks, halo_blocks - 1), 0)
    full = lambda i: (0, 0)

    def wspec(a):
        return pl.BlockSpec(a.shape, full)

    kern = functools.partial(_mix_out_kernel, tiles_per_seq=nt, seq_len=seq_len)
    return pl.pallas_call(
        kern,
        grid=(n // t,),
        in_specs=[pl.BlockSpec((t, D_MODEL), row),
                  pl.BlockSpec((t, MIX_POOL), row),
                  pl.BlockSpec((HALO, MIX_POOL), prev),
                  pl.BlockSpec((HALO, MIX_POOL), nxt),
                  pl.BlockSpec((t, 2 * MIX_CONV), row),
                  pl.BlockSpec((HALO, 2 * MIX_CONV), prev),
                  pl.BlockSpec((HALO, 2 * MIX_CONV), nxt),
                  pl.BlockSpec((t, MIX_LRU), row),
                  pl.BlockSpec((t, MIX_LRU), row),
                  pl.BlockSpec((t, MIX_LRU), row)] + [wspec(a) for a in wts],
        out_specs=[pl.BlockSpec((t, D_MODEL), row),
                   pl.BlockSpec((t, D_MODEL), row),
                   pl.BlockSpec((N_EXPERTS, t), lambda i: (0, i))],
        out_shape=[jax.ShapeDtypeStruct((n, D_MODEL), F32),
                   jax.ShapeDtypeStruct((n, D_MODEL), BF16),
                   jax.ShapeDtypeStruct((N_EXPERTS, n), F32)],
        scratch_shapes=[pltpu.VMEM((t + 2 * HALO, MIX_POOL), F32),
                        pltpu.VMEM((t + 2 * HALO, MIX_CONV), F32)],
        compiler_params=_cparams("parallel"),
        name="mix_out",
    )(x, zp, zp, zp, zc, zc, zc, zg, yf, yb, *wts)


def _select_kernel(aff_ref, gpos_ref, tot_ref, *, capacity, n_tokens):
    v = aff_ref[...]
    nb = v.shape[0]
    cap = jnp.float32(capacity)

    def count(mask):
        return jnp.sum(jnp.where(mask, 1.0, 0.0), keepdims=True)

    def as_f32(bits):
        return lax.bitcast_convert_type(bits, F32)

    bits = jnp.zeros((1, 1), jnp.int32)
    for b in range(30, -1, -1):
        cand = bits | (1 << b)
        bits = jnp.where(count(v >= as_f32(cand)) >= cap, cand, bits)
    above = v >= as_f32(bits + 1)
    tie = (v >= as_f32(bits)) & jnp.logical_not(above)
    need = cap - count(above)

    idx = (lax.broadcasted_iota(jnp.int32, v.shape, 0) * TOK_BLOCK
           + lax.broadcasted_iota(jnp.int32, v.shape, 1))
    last = jnp.zeros((1, 1), jnp.int32)
    for b in range((n_tokens - 1).bit_length() - 1, -1, -1):
        cand = last | (1 << b)
        last = jnp.where(count(tie & (idx < cand)) < need, cand, last)
    sel = above | (tie & (idx <= last))

    sel16 = jnp.where(sel, 1.0, 0.0).astype(BF16)
    r = lax.broadcasted_iota(jnp.int32, (TOK_BLOCK, TOK_BLOCK), 0)
    c = lax.broadcasted_iota(jnp.int32, (TOK_BLOCK, TOK_BLOCK), 1)
    before = jnp.where(r < c, 1.0, 0.0).astype(BF16)
    within = _dot(sel16, before)
    tot = _dot(sel16, jnp.ones((TOK_BLOCK, TOK_BLOCK), BF16))
    rb = lax.broadcasted_iota(jnp.int32, (nb, nb), 0)
    cb = lax.broadcasted_iota(jnp.int32, (nb, nb), 1)
    earlier = jnp.where(cb < rb, 1.0, 0.0).astype(BF16)
    base = _dot(earlier, tot.astype(BF16))
    gpos_ref[...] = jnp.where(sel, base + within, -1.0)
    tot_ref[...] = tot[:, :128]


def _select(aff3, capacity):
    e, nb, _ = aff3.shape
    kern = functools.partial(_select_kernel, capacity=capacity, n_tokens=nb * TOK_BLOCK)
    return pl.pallas_call(
        kern,
        grid=(e,),
        in_specs=[pl.BlockSpec((None, nb, TOK_BLOCK), lambda i: (i, 0, 0))],
        out_specs=[pl.BlockSpec((None, nb, TOK_BLOCK), lambda i: (i, 0, 0)),
                   pl.BlockSpec((None, nb, 128), lambda i: (i, 0, 0))],
        out_shape=[jax.ShapeDtypeStruct((e, nb, TOK_BLOCK), F32),
                   jax.ShapeDtypeStruct((e, nb, 128), F32)],
        compiler_params=_cparams("parallel"),
        name="select",
    )(aff3)


def _routing_plan(cnt, slot_chunk, n_chunks):
    n_exp, nb = cnt.shape
    start = jnp.cumsum(cnt, axis=1) - cnt
    has = cnt > 0
    first_chunk = start // slot_chunk
    last_chunk = jnp.where(has, (start + cnt - 1) // slot_chunk, first_chunk - 1)
    npairs = last_chunk - first_chunk + 1

    ns = n_chunks + nb
    incl = jnp.cumsum(npairs, axis=1)
    total = incl[:, -1:]
    k = jnp.minimum(jnp.arange(ns, dtype=jnp.int32)[None, :], total - 1)
    blk = jax.vmap(lambda a, q: jnp.searchsorted(a, q, side="right"))(incl, k).astype(jnp.int32)
    blk = jnp.minimum(blk, nb - 1)
    take = lambda a: jnp.take_along_axis(a, blk, axis=1)
    chunk = take(first_chunk) + (k - (take(incl) - take(npairs)))
    valid = jnp.arange(ns, dtype=jnp.int32)[None, :] < total
    prev_chunk = jnp.concatenate([jnp.full((n_exp, 1), -1, jnp.int32), chunk[:, :-1]], axis=1)
    next_chunk = jnp.concatenate([chunk[:, 1:], jnp.full((n_exp, 1), -1, jnp.int32)], axis=1)
    is_last_step = jnp.arange(ns, dtype=jnp.int32)[None, :] == total - 1
    first = valid & (chunk != prev_chunk)
    last = valid & ((chunk != next_chunk) | is_last_step)
    d_flags = valid.astype(jnp.int32) + 2 * first.astype(jnp.int32) + 4 * last.astype(jnp.int32)
    disp = (blk.reshape(-1), chunk.reshape(-1).astype(jnp.int32), d_flags.reshape(-1))

    nt = n_exp * (n_chunks + nb - 1) + nb
    np_t = npairs.T
    per_block = jnp.sum(np_t, axis=1)
    steps = jnp.maximum(per_block, 1)
    incl_b = jnp.cumsum(steps)
    total_b = incl_b[-1]
    kk = jnp.arange(nt, dtype=jnp.int32)
    kc = jnp.minimum(kk, total_b - 1)
    bi = jnp.minimum(jnp.searchsorted(incl_b, kc, side="right").astype(jnp.int32), nb - 1)
    q = kc - (incl_b[bi] - steps[bi])
    incl_e = jnp.cumsum(np_t, axis=1)
    ei = jax.vmap(lambda a, s: jnp.searchsorted(a, s, side="right"))(incl_e[bi], q).astype(jnp.int32)
    ei = jnp.minimum(ei, n_exp - 1)
    cj = first_chunk[ei, bi] + (q - (incl_e[bi, ei] - np_t[bi, ei]))
    cj = jnp.clip(cj, 0, n_chunks - 1).astype(jnp.int32)
    live = kk < total_b
    c_valid = live & (per_block[bi] > 0)
    c_first = live & (q == 0)
    c_last = live & (q == steps[bi] - 1)
    c_flags = c_valid.astype(jnp.int32) + 2 * c_first.astype(jnp.int32) + 4 * c_last.astype(jnp.int32)
    comb = (bi, ei, cj, c_flags)
    return disp, comb


def _dispatch_kernel(blk_ref, chunk_ref, flag_ref, gpos_ref, x_ref, xe_ref, acc_ref, *, slot_chunk):
    s = pl.program_id(0)
    flags = flag_ref[s]

    @pl.when((flags & 2) != 0)
    def _():
        acc_ref[...] = jnp.zeros_like(acc_ref)

    @pl.when((flags & 1) != 0)
    def _():
        rel = gpos_ref[...] - (chunk_ref[s] * slot_chunk).astype(F32)
        slot = lax.broadcasted_iota(jnp.int32, (slot_chunk, TOK_BLOCK), 0).astype(F32)
        onehot = jnp.where(slot == rel, 1.0, 0.0).astype(BF16)
        acc_ref[...] += _dot(onehot, x_ref[...])

    @pl.when((flags & 4) != 0)
    def _():
        xe_ref[...] = acc_ref[...].astype(BF16)


def _dispatch(plan, gpos4, xn, capacity, slot_chunk):
    blk, chunk, flags = plan
    n_exp, nb = gpos4.shape[0], gpos4.shape[1]
    n_chunks = capacity // slot_chunk
    ns = n_chunks + nb
    grid_spec = pltpu.PrefetchScalarGridSpec(
        num_scalar_prefetch=3,
        grid=(n_exp * ns,),
        in_specs=[pl.BlockSpec((None, None, 1, TOK_BLOCK), lambda s, b, c, f: (s // ns, b[s], 0, 0)),
                  pl.BlockSpec((TOK_BLOCK, D_MODEL), lambda s, b, c, f: (b[s], 0))],
        out_specs=pl.BlockSpec((slot_chunk, D_MODEL), lambda s, b, c, f: ((s // ns) * n_chunks + c[s], 0)),
        scratch_shapes=[pltpu.VMEM((slot_chunk, D_MODEL), F32)],
    )
    return pl.pallas_call(
        functools.partial(_dispatch_kernel, slot_chunk=slot_chunk),
        grid_spec=grid_spec,
        out_shape=jax.ShapeDtypeStruct((n_exp * capacity, D_MODEL), BF16),
        compiler_params=_cparams("arbitrary"),
        name="dispatch",
    )(blk, chunk, flags, gpos4, xn)


def _ffn_kernel(xe_ref, wg_ref, wu_ref, wd_ref, ye_ref):
    xe = xe_ref[...]
    g = _dot(xe, wg_ref[...])
    u = _dot(xe, wu_ref[...])
    h = (g * jax.nn.sigmoid(g)) * u
    ye_ref[...] = _dot(h.astype(BF16), wd_ref[...]).astype(BF16)


def _ffn(xe, wg, wu, wd, capacity):
    n_exp = wg.shape[0]
    tf = min(FFN_ROWS, capacity)
    per = capacity // tf
    return pl.pallas_call(
        _ffn_kernel,
        grid=(n_exp, per),
        in_specs=[pl.BlockSpec((tf, D_MODEL), lambda e, i: (e * per + i, 0)),
                  pl.BlockSpec((None, D_MODEL, D_FF), lambda e, i: (e, 0, 0)),
                  pl.BlockSpec((None, D_MODEL, D_FF), lambda e, i: (e, 0, 0)),
                  pl.BlockSpec((None, D_FF, D_MODEL), lambda e, i: (e, 0, 0))],
        out_specs=pl.BlockSpec((tf, D_MODEL), lambda e, i: (e * per + i, 0)),
        out_shape=jax.ShapeDtypeStruct(xe.shape, BF16),
        compiler_params=_cparams("parallel", "parallel"),
        name="ffn",
    )(xe, wg, wu, wd)


def _combine_kernel(blk_ref, exp_ref, chunk_ref, flag_ref, gpos_ref, aff_ref, ye_ref, x1_ref,
                    fg_ref, out_ref, acc_ref, *, slot_chunk, final_norm):
    s = pl.program_id(0)
    flags = flag_ref[s]

    @pl.when((flags & 2) != 0)
    def _():
        acc_ref[...] = jnp.zeros_like(acc_ref)

    @pl.when((flags & 1) != 0)
    def _():
        lane = lax.broadcasted_iota(jnp.int32, gpos_ref.shape, 1)
        mine = lane == exp_ref[s]
        pos = jnp.sum(jnp.where(mine, gpos_ref[...], 0.0), axis=1, keepdims=True)
        gate = jnp.sum(jnp.where(mine, aff_ref[...], 0.0), axis=1, keepdims=True)
        rel = pos - (chunk_ref[s] * slot_chunk).astype(F32)
        slot = lax.broadcasted_iota(jnp.int32, (TOK_BLOCK, slot_chunk), 1).astype(F32)
        onehot = jnp.where(slot == rel, 1.0, 0.0).astype(BF16)
        acc_ref[...] += gate * _dot(onehot, ye_ref[...])

    @pl.when((flags & 4) != 0)
    def _():
        y = x1_ref[...] + acc_ref[...]
        if final_norm:
            y = _rms(y, fg_ref[...])
        out_ref[...] = y


def _combine(plan, gpos_tok, aff_tok, ye, x1, final_g, capacity, slot_chunk, final_norm):
    blk, exp, chunk, flags = plan
    n = x1.shape[0]
    n_chunks = capacity // slot_chunk
    nt = blk.shape[0]
    tok = lambda s, b, e, c, f: (b[s], 0)
    grid_spec = pltpu.PrefetchScalarGridSpec(
        num_scalar_prefetch=4,
        grid=(nt,),
        in_specs=[pl.BlockSpec((TOK_BLOCK, N_EXPERTS), tok),
                  pl.BlockSpec((TOK_BLOCK, N_EXPERTS), tok),
                  pl.BlockSpec((slot_chunk, D_MODEL), lambda s, b, e, c, f: (e[s] * n_chunks + c[s], 0)),
                  pl.BlockSpec((TOK_BLOCK, D_MODEL), tok),
                  pl.BlockSpec((1, D_MODEL), lambda s, b, e, c, f: (0, 0))],
        out_specs=pl.BlockSpec((TOK_BLOCK, D_MODEL), tok),
        scratch_shapes=[pltpu.VMEM((TOK_BLOCK, D_MODEL), F32)],
    )
    return pl.pallas_call(
        functools.partial(_combine_kernel, slot_chunk=slot_chunk, final_norm=final_norm),
        grid_spec=grid_spec,
        out_shape=jax.ShapeDtypeStruct((n, D_MODEL), F32),
        compiler_params=_cparams("arbitrary"),
        name="combine",
    )(blk, exp, chunk, flags, gpos_tok, aff_tok, ye, x1, final_g)


def _block_diag(w):
    g, d, _ = w.shape
    eye = jnp.eye(g, dtype=w.dtype)
    return (eye[:, None, :, None] * w[:, :, None, :]).reshape(g * d, g * d)


def _group_mean_matrix(channels, groups):
    gid = jnp.arange(channels) // (channels // groups)
    return jnp.where(gid[:, None] == gid[None, :], groups / channels, 0.0).astype(BF16)


def _prepare_layer(l, p):
    row = lambda a: a.reshape(1, -1)
    rw_t = p["router_w"][l].T
    rw_hi = rw_t.astype(BF16)
    rw_lo = (rw_t - rw_hi.astype(F32)).astype(BF16)
    lru = []
    for d in range(2):
        lru.append((p["lru_conv_w"][l], row(p["lru_conv_b"][l]),
                    _block_diag(p["lru_wa"][l, d]).astype(BF16), row(p["lru_ba"][l, d]),
                    _block_diag(p["lru_wx"][l, d]).astype(BF16), row(p["lru_bx"][l, d]),
                    row(p["lru_lambda"][l, d])))
    mix = (_block_diag(p["pool_w"][l]).astype(BF16), row(p["pool_scale"][l]),
           p["conv_dw_w"][l], row(p["conv_dw_b"][l]), row(p["conv_gn_g"][l]), row(p["conv_gn_b"][l]),
           _group_mean_matrix(MIX_CONV, CONV_GROUPS),
           p["conv_pw_w"][l].astype(BF16), row(p["conv_pw_b"][l]),
           p["w_out"][l].astype(BF16), row(p["norm2_g"][l]), rw_hi, rw_lo)
    return dict(norm1_g=row(p["norm1_g"][l]), w_in=p["w_in"][l].astype(BF16), lru=lru, mix=mix,
                wg=p["exp_w_gate"][l].astype(BF16), wu=p["exp_w_up"][l].astype(BF16),
                wd=p["exp_w_down"][l].astype(BF16))


def _layer(x, lw, final_g, *, seq_len, final_norm):
    n = x.shape[0]
    zp, zl, zg, zc = _in_proj(x, lw["norm1_g"], lw["w_in"])
    yf = _lru_scan(zl, *lw["lru"][0], seq_len=seq_len, reverse=False)
    yb = _lru_scan(zl, *lw["lru"][1], seq_len=seq_len, reverse=True)
    x1, xn, aff = _mix_out(x, zp, zc, zg, yf, yb, lw["mix"], seq_len=seq_len)

    capacity = max(1, CAPACITY_FACTOR * n // N_EXPERTS)
    slot_chunk = min(SLOT_CHUNK, capacity)
    nb = n // TOK_BLOCK
    gpos, tot = _select(aff.reshape(N_EXPERTS, nb, TOK_BLOCK), capacity)
    cnt = tot[:, :, 0].astype(jnp.int32)
    disp_plan, comb_plan = _routing_plan(cnt, slot_chunk, capacity // slot_chunk)
    xe = _dispatch(disp_plan, gpos.reshape(N_EXPERTS, nb, 1, TOK_BLOCK), xn, capacity, slot_chunk)
    ye = _ffn(xe, lw["wg"], lw["wu"], lw["wd"], capacity)
    gpos_tok = gpos.reshape(N_EXPERTS, n).T
    aff_tok = aff.T
    return _combine(comb_plan, gpos_tok, aff_tok, ye, x1, final_g, capacity, slot_chunk, final_norm)


def _trunk(x, layers, final_g):
    b, s, d = x.shape
    h = x.reshape(b * s, d)
    for l, lw in enumerate(layers):
        h = _layer(h, lw, final_g, seq_len=s, final_norm=(l == len(layers) - 1))
    return h.reshape(b, s, d)


def kernel(x_prompt, x_sample, norm1_g, w_in, pool_w, pool_scale, lru_conv_w, lru_conv_b, lru_wa, lru_ba, lru_wx, lru_bx, lru_lambda, conv_dw_w, conv_dw_b, conv_gn_g, conv_gn_b, conv_pw_w, conv_pw_b, w_out, norm2_g, router_w, exp_w_gate, exp_w_up, exp_w_down, final_g):
    p = dict(norm1_g=norm1_g, w_in=w_in, pool_w=pool_w, pool_scale=pool_scale, lru_conv_w=lru_conv_w,
             lru_conv_b=lru_conv_b, lru_wa=lru_wa, lru_ba=lru_ba, lru_wx=lru_wx, lru_bx=lru_bx,
             lru_lambda=lru_lambda, conv_dw_w=conv_dw_w, conv_dw_b=conv_dw_b, conv_gn_g=conv_gn_g,
             conv_gn_b=conv_gn_b, conv_pw_w=conv_pw_w, conv_pw_b=conv_pw_b, w_out=w_out,
             norm2_g=norm2_g, router_w=router_w, exp_w_gate=exp_w_gate, exp_w_up=exp_w_up,
             exp_w_down=exp_w_down)
    layers = [_prepare_layer(l, p) for l in range(norm1_g.shape[0])]
    fg = final_g.reshape(1, -1)
    return (_trunk(x_prompt, layers, fg), _trunk(x_sample, layers, fg))
```

```python
import functools

import jax
import jax.numpy as jnp
from jax import lax
from jax.experimental import pallas as pl
from jax.experimental.pallas import tpu as pltpu

F32 = jnp.float32
BF16 = jnp.bfloat16

D_MODEL = 1024
MIX_POOL = 256
MIX_LRU = 512
MIX_CONV = 256
IN_COLS = MIX_POOL + 2 * MIX_LRU + 2 * MIX_CONV
POOL_GROUP_WIDTH = 64
POOL_MAX_HALF = 8
LRU_CONV_WIDTH = 4
RG_C = 8.0
CONV_WIDTH = 31
CONV_GROUPS = 4
N_EXPERTS = 16
CAPACITY_FACTOR = 2
D_FF = 11 * D_MODEL // 8
RMS_EPS = 1e-6
GN_EPS = 1e-5

SUBLANES = 8
ROW_TILE = 512
SCAN_TILE = 512
HALO = 16
TOK_BLOCK = 256
LANES = 128
BF16_ROWS = 16
COMPACT_WIN = 72
FFN_ROWS = 1024
FFN_SUB = 512
COMBINE_WIN = 128
COMBINE_PASSES = 3
LANE_LOCAL = 64
LANE_BLOCK = 65
VMEM_LIMIT = 48 * 1024 * 1024


def _cparams(*sem):
    return pltpu.CompilerParams(dimension_semantics=sem, vmem_limit_bytes=VMEM_LIMIT)


def _rms(x, g):
    return x * lax.rsqrt(jnp.mean(x * x, axis=-1, keepdims=True) + RMS_EPS) * g


def _split_bf16(x):
    hi = x.astype(BF16)
    lo = (x - hi.astype(F32)).astype(BF16)
    return hi, lo


def _dot(a, b):
    return jnp.dot(a, b, preferred_element_type=F32)


def _in_proj_kernel(x_ref, g_ref, w_ref, zp_ref, zl_ref, zg_ref, zc_ref):
    h = _rms(x_ref[...], g_ref[...])
    z = _dot(h.astype(BF16), w_ref[...])
    o1 = MIX_POOL
    o2 = o1 + MIX_LRU
    o3 = o2 + MIX_LRU
    zp_ref[...] = z[:, :o1]
    zl_ref[...] = z[:, o1:o2]
    zg_ref[...] = z[:, o2:o3]
    zc_ref[...] = z[:, o3:]


def _in_proj(x, g, w_bf16):
    n = x.shape[0]
    tm = min(ROW_TILE, n)
    row = lambda i: (i, 0)
    full = lambda i: (0, 0)
    widths = (MIX_POOL, MIX_LRU, MIX_LRU, 2 * MIX_CONV)
    return pl.pallas_call(
        _in_proj_kernel,
        grid=(n // tm,),
        in_specs=[pl.BlockSpec((tm, D_MODEL), row),
                  pl.BlockSpec((1, D_MODEL), full),
                  pl.BlockSpec((D_MODEL, IN_COLS), full)],
        out_specs=[pl.BlockSpec((tm, w), row) for w in widths],
        out_shape=[jax.ShapeDtypeStruct((n, w), F32) for w in widths],
        compiler_params=_cparams("parallel"),
        name="in_proj",
    )(x, g, w_bf16)


def _lru_scan_kernel(u_ref, prev_ref, next_ref, cw_ref, cb_ref, wa_ref, ba_ref, wx_ref, bx_ref,
                     lam_ref, y_ref, ext_ref, a_ref, b_ref, h_ref, *, tiles_per_seq, reverse):
    t = u_ref.shape[0]
    j = pl.program_id(1)
    jj = tiles_per_seq - 1 - j if reverse else j

    @pl.when(j == 0)
    def _():
        h_ref[...] = jnp.zeros_like(h_ref)

    ext_ref[0:SUBLANES, :] = jnp.where(jj > 0, prev_ref[...], 0.0)
    ext_ref[SUBLANES:SUBLANES + t, :] = u_ref[...]
    ext_ref[SUBLANES + t:2 * SUBLANES + t, :] = jnp.where(jj < tiles_per_seq - 1, next_ref[...], 0.0)

    pad_lo = LRU_CONV_WIDTH // 2
    c = cb_ref[...]
    for k in range(LRU_CONV_WIDTH):
        off = SUBLANES - pad_lo + k
        c = c + cw_ref[k:k + 1, :] * ext_ref[off:off + t, :]
    cb16 = c.astype(BF16)
    r = jax.nn.sigmoid(_dot(cb16, wa_ref[...]) + ba_ref[...])
    i = jax.nn.sigmoid(_dot(cb16, wx_ref[...]) + bx_ref[...])
    neg_lam = -lam_ref[...]
    softplus = jnp.maximum(neg_lam, 0.0) + jnp.log1p(jnp.exp(-jnp.abs(neg_lam)))
    log_a = -RG_C * r * softplus
    th = jnp.tanh(log_a)
    one_minus_a2 = -2.0 * th / (1.0 - th)
    a_ref[...] = jnp.exp(log_a)
    b_ref[...] = jnp.sqrt(one_minus_a2) * (i * c)

    groups = t // SUBLANES
    sub = lax.broadcasted_iota(jnp.int32, (SUBLANES, MIX_LRU), 0)

    def body(g, carry):
        gi = groups - 1 - g if reverse else g
        start = pl.multiple_of(gi * SUBLANES, SUBLANES)
        a8 = a_ref[pl.ds(start, SUBLANES), :]
        b8 = b_ref[pl.ds(start, SUBLANES), :]
        for k in (1, 2, 4):
            shift = SUBLANES - k if reverse else k
            m = (sub < SUBLANES - k) if reverse else (sub >= k)
            a_sh = pltpu.roll(a8, shift, axis=0)
            b_sh = pltpu.roll(b8, shift, axis=0)
            b8 = jnp.where(m, a8 * b_sh + b8, b8)
            a8 = jnp.where(m, a8 * a_sh, a8)
        h = a8 * carry + b8
        y_ref[pl.ds(start, SUBLANES), :] = h
        edge = h[0:1, :] if reverse else h[SUBLANES - 1:SUBLANES, :]
        return jnp.broadcast_to(edge, (SUBLANES, MIX_LRU))

    h_ref[...] = lax.fori_loop(0, groups, body, h_ref[...], unroll=4)


def _lru_scan(z_lru, cw, cb, wa_bd, ba, wx_bd, bx, lam, *, seq_len, reverse):
    n = z_lru.shape[0]
    t = min(SCAN_TILE, seq_len)
    nt = seq_len // t
    nseq = n // seq_len
    halo_blocks = n // SUBLANES
    tile_blocks = t // SUBLANES

    def tile_of(s, j):
        return s * nt + ((nt - 1 - j) if reverse else j)

    def main_map(s, j):
        return (tile_of(s, j), 0)

    def prev_map(s, j):
        return (jnp.maximum(tile_of(s, j) * tile_blocks - 1, 0), 0)

    def next_map(s, j):
        return (jnp.minimum((tile_of(s, j) + 1) * tile_blocks, halo_blocks - 1), 0)

    full = lambda s, j: (0, 0)
    vec = pl.BlockSpec((1, MIX_LRU), full)
    mat = pl.BlockSpec((MIX_LRU, MIX_LRU), full)
    kern = functools.partial(_lru_scan_kernel, tiles_per_seq=nt, reverse=reverse)
    return pl.pallas_call(
        kern,
        grid=(nseq, nt),
        in_specs=[pl.BlockSpec((t, MIX_LRU), main_map),
                  pl.BlockSpec((SUBLANES, MIX_LRU), prev_map),
                  pl.BlockSpec((SUBLANES, MIX_LRU), next_map),
                  pl.BlockSpec((LRU_CONV_WIDTH, MIX_LRU), full), vec, mat, vec, mat, vec, vec],
        out_specs=pl.BlockSpec((t, MIX_LRU), main_map),
        out_shape=jax.ShapeDtypeStruct((n, MIX_LRU), F32),
        scratch_shapes=[pltpu.VMEM((t + 2 * SUBLANES, MIX_LRU), F32),
                        pltpu.VMEM((t, MIX_LRU), F32),
                        pltpu.VMEM((t, MIX_LRU), F32),
                        pltpu.VMEM((SUBLANES, MIX_LRU), F32)],
        compiler_params=_cparams("arbitrary", "arbitrary"),
        name="lru_scan_bwd" if reverse else "lru_scan_fwd",
    )(z_lru, z_lru, z_lru, cw, cb, wa_bd, ba, wx_bd, bx, lam)


def _gelu_tanh(x):
    return 0.5 * x * (1.0 + jnp.tanh(0.7978845608028654 * (x + 0.044715 * (x * x * x))))


def _mix_out_kernel(x_ref, zp_ref, zp_prev_ref, zp_next_ref, zc_ref, zc_prev_ref, zc_next_ref,
                    zg_ref, yf_ref, yb_ref,
                    pool_w_ref, pool_s_ref, dw_w_ref, dw_b_ref, gn_g_ref, gn_b_ref, gavg_ref,
                    pw_w_ref, pw_b_ref, wout_ref, g2_ref, rw_hi_ref, rw_lo_ref,
                    x1_ref, xn_ref, aff_ref,
                    pext_ref, vext_ref, *, tiles_per_seq, seq_len):
    t = x_ref.shape[0]
    jj = pl.program_id(0) % tiles_per_seq
    has_prev = jj > 0
    has_next = jj < tiles_per_seq - 1

    pext_ref[0:HALO, :] = jnp.where(has_prev, zp_prev_ref[...], 0.0)
    pext_ref[HALO:HALO + t, :] = zp_ref[...]
    pext_ref[HALO + t:2 * HALO + t, :] = jnp.where(has_next, zp_next_ref[...], 0.0)
    lane = lax.broadcasted_iota(jnp.int32, (1, MIX_POOL), 1)
    half = jnp.left_shift(1, lane // POOL_GROUP_WIDTH)
    acc = jnp.zeros((t, MIX_POOL), F32)
    for d in range(-POOL_MAX_HALF, POOL_MAX_HALF):
        in_window = jnp.where((d >= -half) & (d < half), 1.0, 0.0)
        acc = acc + in_window * pext_ref[HALO + d:HALO + d + t, :]
    pos = jj * t + lax.broadcasted_iota(jnp.int32, (t, MIX_POOL), 0)
    count = jnp.minimum(pos + half, seq_len) - jnp.maximum(pos - half, 0)
    dmean = acc / count.astype(F32) - zp_ref[...]
    y_pool = _dot(dmean.astype(BF16), pool_w_ref[...]) * pool_s_ref[...]

    def glu(z):
        return z[:, :MIX_CONV] * jax.nn.sigmoid(z[:, MIX_CONV:])

    vext_ref[0:HALO, :] = glu(jnp.where(has_prev, zc_prev_ref[...], 0.0))
    vext_ref[HALO:HALO + t, :] = glu(zc_ref[...])
    vext_ref[HALO + t:2 * HALO + t, :] = glu(jnp.where(has_next, zc_next_ref[...], 0.0))
    v = jnp.zeros((t, MIX_CONV), F32) + dw_b_ref[...]
    for k in range(CONV_WIDTH):
        off = HALO - CONV_WIDTH // 2 + k
        v = v + dw_w_ref[k:k + 1, :] * vext_ref[off:off + t, :]
    gavg = gavg_ref[...]
    v_hi, v_lo = _split_bf16(v)
    mu = _dot(v_hi, gavg) + _dot(v_lo, gavg)
    dv = v - mu
    sq_hi, sq_lo = _split_bf16(dv * dv)
    var = _dot(sq_hi, gavg) + _dot(sq_lo, gavg)
    yn = dv * lax.rsqrt(var + GN_EPS) * gn_g_ref[...] + gn_b_ref[...]
    sw = yn * jax.nn.sigmoid(yn)
    y_conv = _dot(sw.astype(BF16), pw_w_ref[...]) + pw_b_ref[...]

    y_lru = (yf_ref[...] + yb_ref[...]) * _gelu_tanh(zg_ref[...])

    o1 = MIX_POOL
    o2 = o1 + MIX_LRU
    mixed = (_dot(y_pool.astype(BF16), wout_ref[0:o1, :])
             + _dot(y_lru.astype(BF16), wout_ref[o1:o2, :])
             + _dot(y_conv.astype(BF16), wout_ref[o2:, :]))
    x1 = x_ref[...] + mixed
    x1_ref[...] = x1

    xn = _rms(x1, g2_ref[...])
    xn_ref[...] = xn
    xn_hi, xn_lo = _split_bf16(xn)
    nt_dims = (((1,), (1,)), ((), ()))
    logits = (lax.dot_general(rw_hi_ref[...], xn_hi, nt_dims, preferred_element_type=F32)
              + lax.dot_general(rw_hi_ref[...], xn_lo, nt_dims, preferred_element_type=F32)
              + lax.dot_general(rw_lo_ref[...], xn_hi, nt_dims, preferred_element_type=F32))
    e = jnp.exp(logits - jnp.max(logits, axis=0, keepdims=True))
    aff_ref[...] = e / jnp.sum(e, axis=0, keepdims=True)


def _mix_out(x, zp, zc, zg, yf, yb, wts, *, seq_len):
    n = x.shape[0]
    t = min(ROW_TILE, seq_len)
    nt = seq_len // t
    halo_blocks = n // HALO
    tile_blocks = t // HALO
    row = lambda i: (i, 0)
    prev = lambda i: (jnp.maximum(i * tile_blocks - 1, 0), 0)
    nxt = lambda i: (jnp.minimum((i + 1) * tile_blocks, halo_blocks - 1), 0)
    full = lambda i: (0, 0)

    def wspec(a):
        return pl.BlockSpec(a.shape, full)

    kern = functools.partial(_mix_out_kernel, tiles_per_seq=nt, seq_len=seq_len)
    return pl.pallas_call(
        kern,
        grid=(n // t,),
        in_specs=[pl.BlockSpec((t, D_MODEL), row),
                  pl.BlockSpec((t, MIX_POOL), row),
                  pl.BlockSpec((HALO, MIX_POOL), prev),
                  pl.BlockSpec((HALO, MIX_POOL), nxt),
                  pl.BlockSpec((t, 2 * MIX_CONV), row),
                  pl.BlockSpec((HALO, 2 * MIX_CONV), prev),
                  pl.BlockSpec((HALO, 2 * MIX_CONV), nxt),
                  pl.BlockSpec((t, MIX_LRU), row),
                  pl.BlockSpec((t, MIX_LRU), row),
                  pl.BlockSpec((t, MIX_LRU), row)] + [wspec(a) for a in wts],
        out_specs=[pl.BlockSpec((t, D_MODEL), row),
                   pl.BlockSpec((t, D_MODEL), row),
                   pl.BlockSpec((N_EXPERTS, t), lambda i: (0, i))],
        out_shape=[jax.ShapeDtypeStruct((n, D_MODEL), F32),
                   jax.ShapeDtypeStruct((n, D_MODEL), F32),
                   jax.ShapeDtypeStruct((N_EXPERTS, n), F32)],
        scratch_shapes=[pltpu.VMEM((t + 2 * HALO, MIX_POOL), F32),
                        pltpu.VMEM((t + 2 * HALO, MIX_CONV), F32)],
        compiler_params=_cparams("parallel"),
        name="mix_out",
    )(x, zp, zp, zp, zc, zc, zc, zg, yf, yb, *wts)


def _select_kernel(aff_ref, gpos_ref, tot_ref, *, capacity, n_tokens):
    v = aff_ref[...]
    nb = v.shape[0]
    cap = jnp.float32(capacity)

    def count(mask):
        return jnp.sum(jnp.where(mask, 1.0, 0.0), keepdims=True)

    def as_f32(bits):
        return lax.bitcast_convert_type(bits, F32)

    bits = jnp.zeros((1, 1), jnp.int32)
    for b in range(30, -1, -1):
        cand = bits | (1 << b)
        bits = jnp.where(count(v >= as_f32(cand)) >= cap, cand, bits)
    above = v >= as_f32(bits + 1)
    tie = (v >= as_f32(bits)) & jnp.logical_not(above)
    need = cap - count(above)

    idx = (lax.broadcasted_iota(jnp.int32, v.shape, 0) * TOK_BLOCK
           + lax.broadcasted_iota(jnp.int32, v.shape, 1))
    last = jnp.zeros((1, 1), jnp.int32)
    for b in range((n_tokens - 1).bit_length() - 1, -1, -1):
        cand = last | (1 << b)
        last = jnp.where(count(tie & (idx < cand)) < need, cand, last)
    sel = above | (tie & (idx <= last))

    sel16 = jnp.where(sel, 1.0, 0.0).astype(BF16)
    r = lax.broadcasted_iota(jnp.int32, (TOK_BLOCK, TOK_BLOCK), 0)
    c = lax.broadcasted_iota(jnp.int32, (TOK_BLOCK, TOK_BLOCK), 1)
    before = jnp.where(r < c, 1.0, 0.0).astype(BF16)
    within = _dot(sel16, before)
    tot = _dot(sel16, jnp.ones((TOK_BLOCK, TOK_BLOCK), BF16))
    rb = lax.broadcasted_iota(jnp.int32, (nb, nb), 0)
    cb = lax.broadcasted_iota(jnp.int32, (nb, nb), 1)
    earlier = jnp.where(cb < rb, 1.0, 0.0).astype(BF16)
    base = _dot(earlier, tot.astype(BF16))
    gpos_ref[...] = jnp.where(sel, base + within, -1.0)
    tot_ref[...] = tot[:, :128]


def _select(aff3, capacity):
    e, nb, _ = aff3.shape
    kern = functools.partial(_select_kernel, capacity=capacity, n_tokens=nb * TOK_BLOCK)
    return pl.pallas_call(
        kern,
        grid=(e,),
        in_specs=[pl.BlockSpec((None, nb, TOK_BLOCK), lambda i: (i, 0, 0))],
        out_specs=[pl.BlockSpec((None, nb, TOK_BLOCK), lambda i: (i, 0, 0)),
                   pl.BlockSpec((None, nb, 128), lambda i: (i, 0, 0))],
        out_shape=[jax.ShapeDtypeStruct((e, nb, TOK_BLOCK), F32),
                   jax.ShapeDtypeStruct((e, nb, 128), F32)],
        compiler_params=_cparams("parallel"),
        name="select",
    )(aff3)


def _compact_kernel(base_ref, nwin_ref, gpos_ref, payload_ref, out_ref):
    e = pl.program_id(0)
    nb = gpos_ref.shape[0]
    out_ref[...] = jnp.zeros_like(out_ref)
    slot = lax.broadcasted_iota(jnp.int32, (COMPACT_WIN, TOK_BLOCK), 0).astype(F32)

    def block(b, carry):
        first = (base_ref[e * nb + b] // SUBLANES) * SUBLANES
        rows = payload_ref[pl.ds(pl.multiple_of(b * TOK_BLOCK, TOK_BLOCK), TOK_BLOCK), :]
        gpos = gpos_ref[b]

        def window(w, c):
            start = pl.multiple_of(first + w * COMPACT_WIN, SUBLANES)
            onehot = jnp.where(slot == gpos - start.astype(F32), 1.0, 0.0).astype(BF16)
            out_ref[pl.ds(start, COMPACT_WIN), :] += _dot(onehot, rows)
            return c

        return lax.fori_loop(0, nwin_ref[e * nb + b], window, carry)

    lax.fori_loop(0, nb, block, 0)


def _compact(base, nwin, gpos4, payload, capacity):
    n_exp, nb = gpos4.shape[0], gpos4.shape[1]
    n = payload.shape[0]
    rows = capacity + COMPACT_WIN
    grid_spec = pltpu.PrefetchScalarGridSpec(
        num_scalar_prefetch=2,
        grid=(n_exp,),
        in_specs=[pl.BlockSpec((None, nb, 1, TOK_BLOCK), lambda e, b, w: (e, 0, 0, 0)),
                  pl.BlockSpec((n, LANES), lambda e, b, w: (0, 0))],
        out_specs=pl.BlockSpec((None, rows, LANES), lambda e, b, w: (e, 0, 0)),
    )
    return pl.pallas_call(
        _compact_kernel,
        grid_spec=grid_spec,
        out_shape=jax.ShapeDtypeStruct((n_exp, rows, LANES), F32),
        compiler_params=_cparams("arbitrary"),
        name="compact",
    )(base.reshape(-1), nwin.reshape(-1), gpos4, payload)


def _payload(aff):
    n = aff.shape[1]
    a = aff.T
    p0 = a.astype(BF16)
    r1 = a - p0.astype(F32)
    p1 = r1.astype(BF16)
    p2 = (r1 - p1.astype(F32)).astype(BF16)
    tok = jnp.arange(n, dtype=jnp.int32)
    cols = [p0, p1, p2, jnp.zeros((n, LANE_LOCAL - 3 * N_EXPERTS), BF16),
            (tok % TOK_BLOCK).astype(BF16)[:, None], (tok // TOK_BLOCK).astype(BF16)[:, None],
            jnp.zeros((n, LANES - LANE_BLOCK - 1), BF16)]
    return jnp.concatenate(cols, axis=1)


def _ffn_kernel(idx_hbm, xn_hbm, cmp_ref, wg_ref, wu_ref, wd_ref, ye_ref,
                idx_smem, xbuf, idx_sem, row_sem, *, steps_per_expert, tf):
    e = pl.program_id(0)
    s = e * steps_per_expert + pl.program_id(1)
    total = pl.num_programs(0) * steps_per_expert
    slot = s % 2

    def aligned(v):
        return v if isinstance(v, int) else pl.multiple_of(v, tf)

    def idx_copy(chunk, sl):
        return pltpu.make_async_copy(idx_hbm.at[pl.ds(aligned(chunk * tf), tf)],
                                     idx_smem.at[pl.ds(aligned(sl * tf), tf)], idx_sem.at[sl])

    def start_rows(sl):
        def issue(r, c):
            tok = idx_smem[sl * tf + r]
            pltpu.make_async_copy(xn_hbm.at[pl.ds(tok, 1)], xbuf.at[sl, pl.ds(r, 1)],
                                  row_sem.at[sl]).start()
            return c
        lax.fori_loop(0, tf, issue, 0, unroll=8)

    def wait_rows(sl):
        pltpu.make_async_copy(xn_hbm.at[pl.ds(0, tf)], xbuf.at[sl], row_sem.at[sl]).wait()

    @pl.when(s == 0)
    def _():
        idx_copy(0, 0).start()
        idx_copy(0, 0).wait()
        start_rows(0)

        @pl.when(total > 1)
        def _():
            idx_copy(1, 1).start()

    @pl.when(s + 1 < total)
    def _():
        idx_copy(s + 1, 1 - slot).wait()
        start_rows(1 - slot)

    @pl.when(s + 2 < total)
    def _():
        idx_copy(s + 2, slot).start()

    wait_rows(slot)

    lane = lax.broadcasted_iota(jnp.int32, (tf, LANES), 1)
    gate_lanes = ((lane & (N_EXPERTS - 1)) == e) & (lane < 3 * N_EXPERTS)
    gate = jnp.sum(jnp.where(gate_lanes, cmp_ref[...], 0.0), axis=1, keepdims=True)
    sub = min(FFN_SUB, tf)
    for h0 in range(0, tf, sub):
        xe = xbuf[slot, h0:h0 + sub, :].astype(BF16)
        g = _dot(xe, wg_ref[...])
        u = _dot(xe, wu_ref[...])
        h = (g * jax.nn.sigmoid(g)) * u
        y = _dot(h.astype(BF16), wd_ref[...]) * gate[h0:h0 + sub, :]
        ye_ref[h0:h0 + sub, :] = y.astype(BF16)


def _ffn(idx, xn, cmp, wg, wu, wd, capacity):
    n_exp = wg.shape[0]
    tf = min(FFN_ROWS, capacity)
    per = capacity // tf
    kern = functools.partial(_ffn_kernel, steps_per_expert=per, tf=tf)
    return pl.pallas_call(
        kern,
        grid=(n_exp, per),
        in_specs=[pl.BlockSpec(memory_space=pl.ANY),
                  pl.BlockSpec(memory_space=pl.ANY),
                  pl.BlockSpec((None, tf, LANES), lambda e, i: (e, i, 0)),
                  pl.BlockSpec((None, D_MODEL, D_FF), lambda e, i: (e, 0, 0)),
                  pl.BlockSpec((None, D_MODEL, D_FF), lambda e, i: (e, 0, 0)),
                  pl.BlockSpec((None, D_FF, D_MODEL), lambda e, i: (e, 0, 0))],
        out_specs=pl.BlockSpec((tf, D_MODEL), lambda e, i: (e * per + i, 0)),
        out_shape=jax.ShapeDtypeStruct((n_exp * capacity, D_MODEL), BF16),
        scratch_shapes=[pltpu.SMEM((2 * tf,), jnp.int32),
                        pltpu.VMEM((2, tf, D_MODEL), F32),
                        pltpu.SemaphoreType.DMA((2,)),
                        pltpu.SemaphoreType.DMA((2,))],
        compiler_params=_cparams("arbitrary", "arbitrary"),
        name="ffn",
    )(idx, xn, cmp, wg, wu, wd)


def _combine_kernel(first_ref, npass_ref, grow_ref, spread_ref, ye_hbm, x1_ref, fg_ref, out_ref,
                    buf, buf_extra, acc_ref, sem, sem_extra, *, total_rows, final_norm):
    i = pl.program_id(0)
    nblocks = pl.num_programs(0)
    slot = i % 2
    last_start = total_rows - COMBINE_WIN

    def win_start(blk, e, p):
        return pl.multiple_of(jnp.minimum(first_ref[blk * N_EXPERTS + e] + p * COMBINE_WIN, last_start),
                              BF16_ROWS)

    def fetch(blk, p, dst, dsem):
        for e in range(N_EXPERTS):
            pltpu.make_async_copy(ye_hbm.at[pl.ds(win_start(blk, e, p), COMBINE_WIN)],
                                  dst.at[pl.ds(e * COMBINE_WIN, COMBINE_WIN)], dsem).start()

    def wait(dst, dsem):
        pltpu.make_async_copy(ye_hbm.at[pl.ds(0, N_EXPERTS * COMBINE_WIN)], dst, dsem).wait()

    @pl.when(i == 0)
    def _():
        fetch(0, 0, buf.at[0], sem.at[0])

    @pl.when(i + 1 < nblocks)
    def _():
        fetch(i + 1, 0, buf.at[1 - slot], sem.at[1 - slot])

    v = grow_ref[...]
    hi = jnp.floor(v * (1.0 / 256.0))
    lo = v - hi * 256.0
    spread = spread_ref[...]
    row = _dot(hi.astype(BF16), spread) * 256.0 + _dot(lo.astype(BF16), spread) - 1.0
    width = N_EXPERTS * COMBINE_WIN
    lane = lax.broadcasted_iota(jnp.int32, (1, width), 1)
    lane_blk = lane >> (COMBINE_WIN.bit_length() - 1)
    lane_in = (lane & (COMBINE_WIN - 1)).astype(F32)

    def onehot(p):
        start = jnp.zeros((1, width), F32)
        lower = jnp.zeros((1, width), F32)
        for e in range(N_EXPERTS):
            start = jnp.where(lane_blk == e, win_start(i, e, p).astype(F32), start)
            lower = jnp.where(lane_blk == e,
                              (first_ref[i * N_EXPERTS + e] + p * COMBINE_WIN).astype(F32), lower)
        rel = row - start
        if p > 0:
            rel = jnp.where(row >= lower, rel, -1.0)
        return jnp.where(rel == lane_in, 1.0, 0.0).astype(BF16)

    wait(buf.at[slot], sem.at[slot])
    acc_ref[...] = _dot(onehot(0), buf[slot])
    for p in range(1, COMBINE_PASSES):
        @pl.when(p < npass_ref[i])
        def _():
            fetch(i, p, buf_extra, sem_extra.at[0])
            wait(buf_extra, sem_extra.at[0])
            acc_ref[...] += _dot(onehot(p), buf_extra[...])

    y = x1_ref[...] + acc_ref[...]
    if final_norm:
        y = _rms(y, fg_ref[...])
    out_ref[...] = y


def _combine(first, npass, grow, ye, x1, final_g, final_norm):
    n = x1.shape[0]
    width = N_EXPERTS * COMBINE_WIN
    lane = jnp.arange(width) // COMBINE_WIN
    spread = (jnp.arange(LANES)[:, None] == lane[None, :]).astype(BF16)
    tok = lambda i, f, p: (i, 0)
    full = lambda i, f, p: (0, 0)
    grid_spec = pltpu.PrefetchScalarGridSpec(
        num_scalar_prefetch=2,
        grid=(n // TOK_BLOCK,),
        in_specs=[pl.BlockSpec((TOK_BLOCK, LANES), tok),
                  pl.BlockSpec((LANES, width), full),
                  pl.BlockSpec(memory_space=pl.ANY),
                  pl.BlockSpec((TOK_BLOCK, D_MODEL), tok),
                  pl.BlockSpec((1, D_MODEL), full)],
        out_specs=pl.BlockSpec((TOK_BLOCK, D_MODEL), tok),
        scratch_shapes=[pltpu.VMEM((2, width, D_MODEL), BF16),
                        pltpu.VMEM((width, D_MODEL), BF16),
                        pltpu.VMEM((TOK_BLOCK, D_MODEL), F32),
                        pltpu.SemaphoreType.DMA((2,)),
                        pltpu.SemaphoreType.DMA((1,))],
    )
    kern = functools.partial(_combine_kernel, total_rows=ye.shape[0], final_norm=final_norm)
    return pl.pallas_call(
        kern,
        grid_spec=grid_spec,
        out_shape=jax.ShapeDtypeStruct((n, D_MODEL), F32),
        compiler_params=_cparams("arbitrary"),
        name="combine",
    )(first.reshape(-1), npass, grow, spread, ye, x1, final_g)


def _block_diag(w):
    g, d, _ = w.shape
    eye = jnp.eye(g, dtype=w.dtype)
    return (eye[:, None, :, None] * w[:, :, None, :]).reshape(g * d, g * d)


def _group_mean_matrix(channels, groups):
    gid = jnp.arange(channels) // (channels // groups)
    return jnp.where(gid[:, None] == gid[None, :], groups / channels, 0.0).astype(BF16)


def _prepare_layer(l, p):
    row = lambda a: a.reshape(1, -1)
    rw_t = p["router_w"][l].T
    rw_hi = rw_t.astype(BF16)
    rw_lo = (rw_t - rw_hi.astype(F32)).astype(BF16)
    lru = []
    for d in range(2):
        lru.append((p["lru_conv_w"][l], row(p["lru_conv_b"][l]),
                    _block_diag(p["lru_wa"][l, d]).astype(BF16), row(p["lru_ba"][l, d]),
                    _block_diag(p["lru_wx"][l, d]).astype(BF16), row(p["lru_bx"][l, d]),
                    row(p["lru_lambda"][l, d])))
    mix = (_block_diag(p["pool_w"][l]).astype(BF16), row(p["pool_scale"][l]),
           p["conv_dw_w"][l], row(p["conv_dw_b"][l]), row(p["conv_gn_g"][l]), row(p["conv_gn_b"][l]),
           _group_mean_matrix(MIX_CONV, CONV_GROUPS),
           p["conv_pw_w"][l].astype(BF16), row(p["conv_pw_b"][l]),
           p["w_out"][l].astype(BF16), row(p["norm2_g"][l]), rw_hi, rw_lo)
    return dict(norm1_g=row(p["norm1_g"][l]), w_in=p["w_in"][l].astype(BF16), lru=lru, mix=mix,
                wg=p["exp_w_gate"][l].astype(BF16), wu=p["exp_w_up"][l].astype(BF16),
                wd=p["exp_w_down"][l].astype(BF16))


def _layer(x, lw, final_g, *, seq_len, final_norm):
    n = x.shape[0]
    zp, zl, zg, zc = _in_proj(x, lw["norm1_g"], lw["w_in"])
    yf = _lru_scan(zl, *lw["lru"][0], seq_len=seq_len, reverse=False)
    yb = _lru_scan(zl, *lw["lru"][1], seq_len=seq_len, reverse=True)
    x1, xn, aff = _mix_out(x, zp, zc, zg, yf, yb, lw["mix"], seq_len=seq_len)

    capacity = max(1, CAPACITY_FACTOR * n // N_EXPERTS)
    nb = n // TOK_BLOCK
    gpos, tot = _select(aff.reshape(N_EXPERTS, nb, TOK_BLOCK), capacity)

    cnt = tot[:, :, 0].astype(jnp.int32)
    base = jnp.cumsum(cnt, axis=1) - cnt
    nwin = jnp.where(cnt > 0, (base % SUBLANES + cnt + COMPACT_WIN - 1) // COMPACT_WIN, 0)
    cmp = _compact(base, nwin, gpos.reshape(N_EXPERTS, nb, 1, TOK_BLOCK), _payload(aff), capacity)
    idx = cmp[:, :capacity, LANE_BLOCK] * TOK_BLOCK + cmp[:, :capacity, LANE_LOCAL]
    idx = jnp.clip(idx.astype(jnp.int32), 0, n - 1).reshape(-1)
    ye = _ffn(idx, xn, cmp, lw["wg"], lw["wu"], lw["wd"], capacity)

    expert_row0 = jnp.arange(N_EXPERTS, dtype=jnp.int32)[:, None] * capacity
    gpos2 = gpos.reshape(N_EXPERTS, n)
    grow = jnp.where(gpos2 >= 0, gpos2 + expert_row0.astype(F32) + 1.0, 0.0).T
    grow = jnp.pad(grow, ((0, 0), (0, LANES - N_EXPERTS)))
    row0 = expert_row0 + base
    first = (row0 // BF16_ROWS) * BF16_ROWS
    passes = jnp.where(cnt > 0, (row0 % BF16_ROWS + cnt + COMBINE_WIN - 1) // COMBINE_WIN, 1)
    return _combine(first.T, jnp.max(passes, axis=0), grow, ye, x1, final_g, final_norm)


def _trunk(x, layers, final_g):
    b, s, d = x.shape
    h = x.reshape(b * s, d)
    for l, lw in enumerate(layers):
        h = _layer(h, lw, final_g, seq_len=s, final_norm=(l == len(layers) - 1))
    return h.reshape(b, s, d)


def kernel(x_prompt, x_sample, norm1_g, w_in, pool_w, pool_scale, lru_conv_w, lru_conv_b, lru_wa, lru_ba, lru_wx, lru_bx, lru_lambda, conv_dw_w, conv_dw_b, conv_gn_g, conv_gn_b, conv_pw_w, conv_pw_b, w_out, norm2_g, router_w, exp_w_gate, exp_w_up, exp_w_down, final_g):
    p = dict(norm1_g=norm1_g, w_in=w_in, pool_w=pool_w, pool_scale=pool_scale, lru_conv_w=lru_conv_w,
             lru_conv_b=lru_conv_b, lru_wa=lru_wa, lru_ba=lru_ba, lru_wx=lru_wx, lru_bx=lru_bx,
             lru_lambda=lru_lambda, conv_dw_w=conv_dw_w, conv_dw_b=conv_dw_b, conv_gn_g=conv_gn_g,
             conv_gn_b=conv_gn_b, conv_pw_w=conv_pw_w, conv_pw_b=conv_pw_b, w_out=w_out,
             norm2_g=norm2_g, router_w=router_w, exp_w_gate=exp_w_gate, exp_w_up=exp_w_up,
             exp_w_down=exp_w_down)
    layers = [_prepare_layer(l, p) for l in range(norm1_g.shape[0])]
    fg = final_g.reshape(1, -1)
    return (_trunk(x_prompt, layers, fg), _trunk(x_sample, layers, fg))
```

```python
import functools

import jax
import jax.numpy as jnp
from jax import lax
from jax.experimental import pallas as pl
from jax.experimental.pallas import tpu as pltpu

F32 = jnp.float32
BF16 = jnp.bfloat16

D_MODEL = 1024
MIX_POOL = 256
MIX_LRU = 512
MIX_CONV = 256
IN_COLS = MIX_POOL + 2 * MIX_LRU + 2 * MIX_CONV
POOL_GROUP_WIDTH = 64
POOL_MAX_HALF = 8
LRU_CONV_WIDTH = 4
RG_C = 8.0
CONV_WIDTH = 31
CONV_GROUPS = 4
N_EXPERTS = 16
CAPACITY_FACTOR = 2
D_FF = 11 * D_MODEL // 8
RMS_EPS = 1e-6
GN_EPS = 1e-5

SUBLANES = 8
ROW_TILE = 512
SCAN_TILE = 512
HALO = 16
TOK_BLOCK = 256
LANES = 128
BF16_ROWS = 16
COMPACT_WIN = 72
COMPACT_UNROLL = 4
FFN_ROWS = 1024
FFN_SUB = 512
COMBINE_WIN = 128
COMBINE_PASSES = 3
LANE_LOCAL = 64
LANE_BLOCK = 65
VMEM_LIMIT = 48 * 1024 * 1024


def _cparams(*sem):
    return pltpu.CompilerParams(dimension_semantics=sem, vmem_limit_bytes=VMEM_LIMIT)


def _rms(x, g):
    return x * lax.rsqrt(jnp.mean(x * x, axis=-1, keepdims=True) + RMS_EPS) * g


def _split_bf16(x):
    hi = x.astype(BF16)
    lo = (x - hi.astype(F32)).astype(BF16)
    return hi, lo


def _dot(a, b):
    return jnp.dot(a, b, preferred_element_type=F32)


def _in_proj_kernel(x_ref, g_ref, w_ref, zp_ref, zl_ref, zg_ref, zc_ref):
    h = _rms(x_ref[...], g_ref[...])
    z = _dot(h.astype(BF16), w_ref[...])
    o1 = MIX_POOL
    o2 = o1 + MIX_LRU
    o3 = o2 + MIX_LRU
    zp_ref[...] = z[:, :o1]
    zl_ref[...] = z[:, o1:o2]
    zg_ref[...] = z[:, o2:o3]
    zc_ref[...] = z[:, o3:]


def _in_proj(x, g, w_bf16):
    n = x.shape[0]
    tm = min(ROW_TILE, n)
    row = lambda i: (i, 0)
    full = lambda i: (0, 0)
    widths = (MIX_POOL, MIX_LRU, MIX_LRU, 2 * MIX_CONV)
    return pl.pallas_call(
        _in_proj_kernel,
        grid=(n // tm,),
        in_specs=[pl.BlockSpec((tm, D_MODEL), row),
                  pl.BlockSpec((1, D_MODEL), full),
                  pl.BlockSpec((D_MODEL, IN_COLS), full)],
        out_specs=[pl.BlockSpec((tm, w), row) for w in widths],
        out_shape=[jax.ShapeDtypeStruct((n, w), F32) for w in widths],
        compiler_params=_cparams("parallel"),
        name="in_proj",
    )(x, g, w_bf16)


def _shift_rows(x, m):
    return x if m == 0 else pltpu.roll(x, x.shape[0] - m, axis=0)


def _lru_scan_kernel(*refs, tiles_per_seq, reverse):
    if reverse:
        c_ref, wa_ref, ba_ref, wx_ref, bx_ref, lam_ref, y_ref, a_ref, b_ref, h_ref = refs
    else:
        (u_ref, prev_ref, next_ref, cw_ref, cb_ref, wa_ref, ba_ref, wx_ref, bx_ref, lam_ref,
         y_ref, c_ref, a_ref, b_ref, h_ref) = refs
    t = y_ref.shape[0]
    j = pl.program_id(1)

    @pl.when(j == 0)
    def _():
        h_ref[...] = jnp.zeros_like(h_ref)

    if reverse:
        c = c_ref[...]
    else:
        ext = jnp.concatenate([jnp.where(j > 0, prev_ref[...], 0.0), u_ref[...],
                               jnp.where(j < tiles_per_seq - 1, next_ref[...], 0.0)], axis=0)
        pad_lo = LRU_CONV_WIDTH // 2
        c = cb_ref[...]
        for k in range(LRU_CONV_WIDTH):
            c = c + cw_ref[k:k + 1, :] * _shift_rows(ext, SUBLANES - pad_lo + k)[0:t, :]
        c_ref[...] = c
    cb16 = c.astype(BF16)
    r = jax.nn.sigmoid(_dot(cb16, wa_ref[...]) + ba_ref[...])
    i = jax.nn.sigmoid(_dot(cb16, wx_ref[...]) + bx_ref[...])
    neg_lam = -lam_ref[...]
    softplus = jnp.maximum(neg_lam, 0.0) + jnp.log1p(jnp.exp(-jnp.abs(neg_lam)))
    log_a = -RG_C * r * softplus
    th = jnp.tanh(log_a)
    one_minus_a2 = -2.0 * th / (1.0 - th)
    a_ref[...] = jnp.exp(log_a)
    b_ref[...] = jnp.sqrt(one_minus_a2) * (i * c)

    groups = t // SUBLANES
    sub = lax.broadcasted_iota(jnp.int32, (SUBLANES, MIX_LRU), 0)

    def body(g, carry):
        gi = groups - 1 - g if reverse else g
        start = pl.multiple_of(gi * SUBLANES, SUBLANES)
        a8 = a_ref[pl.ds(start, SUBLANES), :]
        b8 = b_ref[pl.ds(start, SUBLANES), :]
        for k in (1, 2, 4):
            shift = SUBLANES - k if reverse else k
            m = (sub < SUBLANES - k) if reverse else (sub >= k)
            a_sh = pltpu.roll(a8, shift, axis=0)
            b_sh = pltpu.roll(b8, shift, axis=0)
            b8 = jnp.where(m, a8 * b_sh + b8, b8)
            a8 = jnp.where(m, a8 * a_sh, a8)
        h = a8 * carry + b8
        y_ref[pl.ds(start, SUBLANES), :] = h
        edge = h[0:1, :] if reverse else h[SUBLANES - 1:SUBLANES, :]
        return jnp.broadcast_to(edge, (SUBLANES, MIX_LRU))

    h_ref[...] = lax.fori_loop(0, groups, body, h_ref[...], unroll=8)


def _lru_scan(src, cw, cb, wa_bd, ba, wx_bd, bx, lam, *, seq_len, reverse):
    n = src.shape[0]
    t = min(SCAN_TILE, seq_len)
    nt = seq_len // t
    nseq = n // seq_len
    halo_blocks = n // SUBLANES
    tile_blocks = t // SUBLANES

    def tile_of(s, j):
        return s * nt + ((nt - 1 - j) if reverse else j)

    def main_map(s, j):
        return (tile_of(s, j), 0)

    def prev_map(s, j):
        return (jnp.maximum(tile_of(s, j) * tile_blocks - 1, 0), 0)

    def next_map(s, j):
        return (jnp.minimum((tile_of(s, j) + 1) * tile_blocks, halo_blocks - 1), 0)

    full = lambda s, j: (0, 0)
    vec = pl.BlockSpec((1, MIX_LRU), full)
    mat = pl.BlockSpec((MIX_LRU, MIX_LRU), full)
    tile = pl.BlockSpec((t, MIX_LRU), main_map)
    out = jax.ShapeDtypeStruct((n, MIX_LRU), F32)
    gate_specs = [mat, vec, mat, vec, vec]
    if reverse:
        in_specs, args = [tile] + gate_specs, (src, wa_bd, ba, wx_bd, bx, lam)
        out_specs, out_shape = tile, out
    else:
        in_specs = [tile, pl.BlockSpec((SUBLANES, MIX_LRU), prev_map),
                    pl.BlockSpec((SUBLANES, MIX_LRU), next_map),
                    pl.BlockSpec((LRU_CONV_WIDTH, MIX_LRU), full), vec] + gate_specs
        args = (src, src, src, cw, cb, wa_bd, ba, wx_bd, bx, lam)
        out_specs, out_shape = [tile, tile], [out, out]
    return pl.pallas_call(
        functools.partial(_lru_scan_kernel, tiles_per_seq=nt, reverse=reverse),
        grid=(nseq, nt),
        in_specs=in_specs,
        out_specs=out_specs,
        out_shape=out_shape,
        scratch_shapes=[pltpu.VMEM((t, MIX_LRU), F32),
                        pltpu.VMEM((t, MIX_LRU), F32),
                        pltpu.VMEM((SUBLANES, MIX_LRU), F32)],
        compiler_params=_cparams("arbitrary", "arbitrary"),
        name="lru_scan_bwd" if reverse else "lru_scan_fwd",
    )(*args)


def _gelu_tanh(x):
    return 0.5 * x * (1.0 + jnp.tanh(0.7978845608028654 * (x + 0.044715 * (x * x * x))))


def _mix_out_kernel(x_ref, zp_ref, zp_prev_ref, zp_next_ref, zc_ref, zc_prev_ref, zc_next_ref,
                    zg_ref, yf_ref, yb_ref,
                    pool_w_ref, pool_s_ref, dw_w_ref, dw_b_ref, gn_g_ref, gn_b_ref, gavg_ref,
                    pw_w_ref, pw_b_ref, wout_ref, g2_ref, rw_hi_ref, rw_lo_ref,
                    x1_ref, xn_ref, aff_ref, *, tiles_per_seq, seq_len):
    t = x_ref.shape[0]
    jj = pl.program_id(0) % tiles_per_seq
    has_prev = jj > 0
    has_next = jj < tiles_per_seq - 1

    pext = jnp.concatenate([jnp.where(has_prev, zp_prev_ref[...], 0.0), zp_ref[...],
                            jnp.where(has_next, zp_next_ref[...], 0.0)], axis=0)
    lane = lax.broadcasted_iota(jnp.int32, (1, MIX_POOL), 1)
    half = jnp.left_shift(1, lane // POOL_GROUP_WIDTH)
    acc = jnp.zeros((t, MIX_POOL), F32)
    for m in range(SUBLANES):
        shifted = _shift_rows(pext, m)
        for off in range(HALO - POOL_MAX_HALF + m, HALO + POOL_MAX_HALF, SUBLANES):
            d = off - HALO
            in_window = jnp.where((d >= -half) & (d < half), 1.0, 0.0)
            acc = acc + in_window * shifted[off - m:off - m + t, :]
    pos = jj * t + lax.broadcasted_iota(jnp.int32, (t, MIX_POOL), 0)
    count = jnp.minimum(pos + half, seq_len) - jnp.maximum(pos - half, 0)
    dmean = acc / count.astype(F32) - zp_ref[...]
    y_pool = _dot(dmean.astype(BF16), pool_w_ref[...]) * pool_s_ref[...]

    def glu(z):
        return z[:, :MIX_CONV] * jax.nn.sigmoid(z[:, MIX_CONV:])

    vext = jnp.concatenate([glu(jnp.where(has_prev, zc_prev_ref[...], 0.0)), glu(zc_ref[...]),
                            glu(jnp.where(has_next, zc_next_ref[...], 0.0))], axis=0)
    v = jnp.zeros((t, MIX_CONV), F32) + dw_b_ref[...]
    first_off = HALO - CONV_WIDTH // 2
    for m in range(SUBLANES):
        shifted = _shift_rows(vext, m)
        for off in range(first_off + (m - first_off) % SUBLANES, first_off + CONV_WIDTH, SUBLANES):
            k = off - first_off
            v = v + dw_w_ref[k:k + 1, :] * shifted[off - m:off - m + t, :]
    gavg = gavg_ref[...]
    v_hi, v_lo = _split_bf16(v)
    mu = _dot(v_hi, gavg) + _dot(v_lo, gavg)
    dv = v - mu
    sq_hi, sq_lo = _split_bf16(dv * dv)
    var = _dot(sq_hi, gavg) + _dot(sq_lo, gavg)
    yn = dv * lax.rsqrt(var + GN_EPS) * gn_g_ref[...] + gn_b_ref[...]
    sw = yn * jax.nn.sigmoid(yn)
    y_conv = _dot(sw.astype(BF16), pw_w_ref[...]) + pw_b_ref[...]

    y_lru = (yf_ref[...] + yb_ref[...]) * _gelu_tanh(zg_ref[...])

    o1 = MIX_POOL
    o2 = o1 + MIX_LRU
    mixed = (_dot(y_pool.astype(BF16), wout_ref[0:o1, :])
             + _dot(y_lru.astype(BF16), wout_ref[o1:o2, :])
             + _dot(y_conv.astype(BF16), wout_ref[o2:, :]))
    x1 = x_ref[...] + mixed
    x1_ref[...] = x1

    xn = _rms(x1, g2_ref[...])
    xn_ref[...] = xn
    xn_hi, xn_lo = _split_bf16(xn)
    nt_dims = (((1,), (1,)), ((), ()))
    logits = (lax.dot_general(rw_hi_ref[...], xn_hi, nt_dims, preferred_element_type=F32)
              + lax.dot_general(rw_hi_ref[...], xn_lo, nt_dims, preferred_element_type=F32)
              + lax.dot_general(rw_lo_ref[...], xn_hi, nt_dims, preferred_element_type=F32))
    e = jnp.exp(logits - jnp.max(logits, axis=0, keepdims=True))
    aff_ref[...] = e / jnp.sum(e, axis=0, keepdims=True)


def _mix_out(x, zp, zc, zg, yf, yb, wts, *, seq_len):
    n = x.shape[0]
    t = min(ROW_TILE, seq_len)
    nt = seq_len // t
    halo_blocks = n // HALO
    tile_blocks = t // HALO
    row = lambda i: (i, 0)
    prev = lambda i: (jnp.maximum(i * tile_blocks - 1, 0), 0)
    nxt = lambda i: (jnp.minimum((i + 1) * tile_blocks, halo_blocks - 1), 0)
    full = lambda i: (0, 0)

    def wspec(a):
        return pl.BlockSpec(a.shape, full)

    kern = functools.partial(_mix_out_kernel, tiles_per_seq=nt, seq_len=seq_len)
    return pl.pallas_call(
        kern,
        grid=(n // t,),
        in_specs=[pl.BlockSpec((t, D_MODEL), row),
                  pl.BlockSpec((t, MIX_POOL), row),
                  pl.BlockSpec((HALO, MIX_POOL), prev),
                  pl.BlockSpec((HALO, MIX_POOL), nxt),
                  pl.BlockSpec((t, 2 * MIX_CONV), row),
                  pl.BlockSpec((HALO, 2 * MIX_CONV), prev),
                  pl.BlockSpec((HALO, 2 * MIX_CONV), nxt),
                  pl.BlockSpec((t, MIX_LRU), row),
                  pl.BlockSpec((t, MIX_LRU), row),
                  pl.BlockSpec((t, MIX_LRU), row)] + [wspec(a) for a in wts],
        out_specs=[pl.BlockSpec((t, D_MODEL), row),
                   pl.BlockSpec((t, D_MODEL), row),
                   pl.BlockSpec((N_EXPERTS, t), lambda i: (0, i))],
        out_shape=[jax.ShapeDtypeStruct((n, D_MODEL), F32),
                   jax.ShapeDtypeStruct((n, D_MODEL), F32),
                   jax.ShapeDtypeStruct((N_EXPERTS, n), F32)],
        compiler_params=_cparams("parallel"),
        name="mix_out",
    )(x, zp, zp, zp, zc, zc, zc, zg, yf, yb, *wts)


def _select_kernel(aff_ref, gpos_ref, tot_ref, *, capacity, n_tokens):
    n_exp, nb = aff_ref.shape[0], aff_ref.shape[1]
    cap = jnp.float32(capacity)
    experts = range(n_exp)

    def count(mask):
        return jnp.sum(jnp.where(mask, 1.0, 0.0), keepdims=True)

    def as_f32(bits):
        return lax.bitcast_convert_type(bits, F32)

    def value_step(it, bits):
        bit = jnp.left_shift(1, 30 - it)
        out = []
        for e in experts:
            cand = bits[e] | bit
            out.append(jnp.where(count(aff_ref[e] >= as_f32(cand)) >= cap, cand, bits[e]))
        return tuple(out)

    zero = jnp.zeros((1, 1), jnp.int32)
    bits = lax.fori_loop(0, 31, value_step, (zero,) * n_exp)

    def masks(e):
        v = aff_ref[e]
        above = v >= as_f32(bits[e] + 1)
        return above, (v >= as_f32(bits[e])) & jnp.logical_not(above)

    need = [cap - count(masks(e)[0]) for e in experts]
    idx = (lax.broadcasted_iota(jnp.int32, (nb, TOK_BLOCK), 0) * TOK_BLOCK
           + lax.broadcasted_iota(jnp.int32, (nb, TOK_BLOCK), 1))
    index_bits = (n_tokens - 1).bit_length()

    def index_step(it, last):
        bit = jnp.left_shift(1, index_bits - 1 - it)
        out = []
        for e in experts:
            cand = last[e] | bit
            out.append(jnp.where(count(masks(e)[1] & (idx < cand)) < need[e], cand, last[e]))
        return tuple(out)

    last = lax.fori_loop(0, index_bits, index_step, (zero,) * n_exp)

    r = lax.broadcasted_iota(jnp.int32, (TOK_BLOCK, TOK_BLOCK), 0)
    c = lax.broadcasted_iota(jnp.int32, (TOK_BLOCK, TOK_BLOCK), 1)
    before = jnp.where(r < c, 1.0, 0.0).astype(BF16)
    ones = jnp.ones((TOK_BLOCK, TOK_BLOCK), BF16)
    rb = lax.broadcasted_iota(jnp.int32, (nb, nb), 0)
    cb = lax.broadcasted_iota(jnp.int32, (nb, nb), 1)
    earlier = jnp.where(cb < rb, 1.0, 0.0).astype(BF16)
    for e in experts:
        above, tie = masks(e)
        sel = above | (tie & (idx <= last[e]))
        sel16 = jnp.where(sel, 1.0, 0.0).astype(BF16)
        within = _dot(sel16, before)
        tot = _dot(sel16, ones)
        base = _dot(earlier, tot.astype(BF16))
        gpos_ref[e] = jnp.where(sel, base + within, -1.0)
        tot_ref[e] = tot[:, :LANES]


def _select(aff3, capacity):
    e, nb, _ = aff3.shape
    kern = functools.partial(_select_kernel, capacity=capacity, n_tokens=nb * TOK_BLOCK)
    whole = lambda i: (0, 0, 0)
    return pl.pallas_call(
        kern,
        grid=(1,),
        in_specs=[pl.BlockSpec((e, nb, TOK_BLOCK), whole)],
        out_specs=[pl.BlockSpec((e, nb, TOK_BLOCK), whole),
                   pl.BlockSpec((e, nb, LANES), whole)],
        out_shape=[jax.ShapeDtypeStruct((e, nb, TOK_BLOCK), F32),
                   jax.ShapeDtypeStruct((e, nb, LANES), F32)],
        compiler_params=_cparams("arbitrary"),
        name="select",
    )(aff3)


def _compact_kernel(base_ref, nwin_ref, gpos_ref, payload_ref, out_ref):
    e = pl.program_id(0)
    nb = gpos_ref.shape[0]
    out_ref[...] = jnp.zeros_like(out_ref)
    slot = lax.broadcasted_iota(jnp.int32, (COMPACT_WIN, TOK_BLOCK), 0).astype(F32)

    def moved(b, w):
        start = pl.multiple_of((base_ref[e * nb + b] // SUBLANES) * SUBLANES + w * COMPACT_WIN, SUBLANES)
        rows = payload_ref[pl.ds(pl.multiple_of(b * TOK_BLOCK, TOK_BLOCK), TOK_BLOCK), :]
        onehot = jnp.where(slot == gpos_ref[b] - start.astype(F32), 1.0, 0.0).astype(BF16)
        return start, _dot(onehot, rows)

    def group(g, carry):
        blocks = [g * COMPACT_UNROLL + u for u in range(COMPACT_UNROLL)]
        for start, rows in [moved(b, 0) for b in blocks]:
            out_ref[pl.ds(start, COMPACT_WIN), :] += rows

        for b in blocks:
            def window(w, c, b=b):
                start, rows = moved(b, w)
                out_ref[pl.ds(start, COMPACT_WIN), :] += rows
                return c
            lax.fori_loop(1, nwin_ref[e * nb + b], window, 0)
        return carry

    lax.fori_loop(0, nb // COMPACT_UNROLL, group, 0)


def _compact(base, nwin, gpos4, payload, capacity):
    n_exp, nb = gpos4.shape[0], gpos4.shape[1]
    n = payload.shape[0]
    rows = capacity + COMPACT_WIN
    grid_spec = pltpu.PrefetchScalarGridSpec(
        num_scalar_prefetch=2,
        grid=(n_exp,),
        in_specs=[pl.BlockSpec((None, nb, 1, TOK_BLOCK), lambda e, b, w: (e, 0, 0, 0)),
                  pl.BlockSpec((n, LANES), lambda e, b, w: (0, 0))],
        out_specs=pl.BlockSpec((None, rows, LANES), lambda e, b, w: (e, 0, 0)),
    )
    return pl.pallas_call(
        _compact_kernel,
        grid_spec=grid_spec,
        out_shape=jax.ShapeDtypeStruct((n_exp, rows, LANES), F32),
        compiler_params=_cparams("arbitrary"),
        name="compact",
    )(base.reshape(-1), nwin.reshape(-1), gpos4, payload)


def _payload(aff):
    n = aff.shape[1]
    a = aff.T
    p0 = a.astype(BF16)
    r1 = a - p0.astype(F32)
    p1 = r1.astype(BF16)
    p2 = (r1 - p1.astype(F32)).astype(BF16)
    tok = jnp.arange(n, dtype=jnp.int32)
    cols = [p0, p1, p2, jnp.zeros((n, LANE_LOCAL - 3 * N_EXPERTS), BF16),
            (tok % TOK_BLOCK).astype(BF16)[:, None], (tok // TOK_BLOCK).astype(BF16)[:, None],
            jnp.zeros((n, LANES - LANE_BLOCK - 1), BF16)]
    return jnp.concatenate(cols, axis=1)


def _ffn_kernel(idx_hbm, xn_hbm, cmp_ref, wg_ref, wu_ref, wd_ref, ye_ref,
                idx_smem, xbuf, idx_sem, row_sem, *, steps_per_expert, tf):
    e = pl.program_id(0)
    s = e * steps_per_expert + pl.program_id(1)
    total = pl.num_programs(0) * steps_per_expert
    slot = s % 2

    def aligned(v):
        return v if isinstance(v, int) else pl.multiple_of(v, tf)

    def idx_copy(chunk, sl):
        return pltpu.make_async_copy(idx_hbm.at[pl.ds(aligned(chunk * tf), tf)],
                                     idx_smem.at[pl.ds(aligned(sl * tf), tf)], idx_sem.at[sl])

    def start_row(sl, r):
        tok = idx_smem[sl * tf + r]
        pltpu.make_async_copy(xn_hbm.at[pl.ds(tok, 1)], xbuf.at[sl, pl.ds(r, 1)],
                              row_sem.at[sl]).start()

    def wait_rows(sl):
        pltpu.make_async_copy(xn_hbm.at[pl.ds(0, tf)], xbuf.at[sl], row_sem.at[sl]).wait()

    @pl.when(s == 0)
    def _():
        idx_copy(0, 0).start()
        idx_copy(0, 0).wait()

        def issue(r, c):
            start_row(0, r)
            return c
        lax.fori_loop(0, tf, issue, 0, unroll=8)
        idx_copy(1, 1).start()

    @pl.when(s + 1 < total)
    def _():
        idx_copy(s + 1, 1 - slot).wait()

    wait_rows(slot)

    lane = lax.broadcasted_iota(jnp.int32, (tf, LANES), 1)
    gate_lanes = ((lane & (N_EXPERTS - 1)) == e) & (lane < 3 * N_EXPERTS)
    gate = jnp.sum(jnp.where(gate_lanes, cmp_ref[...], 0.0), axis=1, keepdims=True)

    sub = min(FFN_SUB, tf)
    pieces = 3 * (tf // sub)
    issued = 0

    def issue_piece(p):
        nonlocal issued
        upto = tf * (p + 1) // pieces
        for r in range(issued, upto):
            start_row(1 - slot, r)
        issued = upto

    for hi, h0 in enumerate(range(0, tf, sub)):
        xe = xbuf[slot, h0:h0 + sub, :].astype(BF16)
        g = _dot(xe, wg_ref[...])
        issue_piece(3 * hi)
        u = _dot(xe, wu_ref[...])
        issue_piece(3 * hi + 1)
        h = (g * jax.nn.sigmoid(g)) * u
        y = _dot(h.astype(BF16), wd_ref[...]) * gate[h0:h0 + sub, :]
        ye_ref[h0:h0 + sub, :] = y.astype(BF16)
        issue_piece(3 * hi + 2)

    @pl.when(s + 2 < total)
    def _():
        idx_copy(s + 2, slot).start()

    @pl.when(s == total - 1)
    def _():
        wait_rows(1 - slot)


def _ffn(idx, xn, cmp, wg, wu, wd, capacity):
    n_exp = wg.shape[0]
    tf = min(FFN_ROWS, capacity)
    per = capacity // tf
    kern = functools.partial(_ffn_kernel, steps_per_expert=per, tf=tf)
    return pl.pallas_call(
        kern,
        grid=(n_exp, per),
        in_specs=[pl.BlockSpec(memory_space=pl.ANY),
                  pl.BlockSpec(memory_space=pl.ANY),
                  pl.BlockSpec((None, tf, LANES), lambda e, i: (e, i, 0)),
                  pl.BlockSpec((None, D_MODEL, D_FF), lambda e, i: (e, 0, 0)),
                  pl.BlockSpec((None, D_MODEL, D_FF), lambda e, i: (e, 0, 0)),
                  pl.BlockSpec((None, D_FF, D_MODEL), lambda e, i: (e, 0, 0))],
        out_specs=pl.BlockSpec((tf, D_MODEL), lambda e, i: (e * per + i, 0)),
        out_shape=jax.ShapeDtypeStruct((n_exp * capacity, D_MODEL), BF16),
        scratch_shapes=[pltpu.SMEM((2 * tf,), jnp.int32),
                        pltpu.VMEM((2, tf, D_MODEL), F32),
                        pltpu.SemaphoreType.DMA((2,)),
                        pltpu.SemaphoreType.DMA((2,))],
        compiler_params=_cparams("arbitrary", "arbitrary"),
        name="ffn",
    )(idx, xn, cmp, wg, wu, wd)


def _combine_kernel(first_ref, npass_ref, grow_ref, spread_ref, ye_hbm, x1_ref, fg_ref, out_ref,
                    buf, buf_extra, acc_ref, sem, sem_extra, *, total_rows, final_norm):
    i = pl.program_id(0)
    nblocks = pl.num_programs(0)
    slot = i % 2
    last_start = total_rows - COMBINE_WIN

    def win_start(blk, e, p):
        return pl.multiple_of(jnp.minimum(first_ref[blk * N_EXPERTS + e] + p * COMBINE_WIN, last_start),
                              BF16_ROWS)

    def fetch(blk, p, dst, dsem):
        for e in range(N_EXPERTS):
            pltpu.make_async_copy(ye_hbm.at[pl.ds(win_start(blk, e, p), COMBINE_WIN)],
                                  dst.at[pl.ds(e * COMBINE_WIN, COMBINE_WIN)], dsem).start()

    def wait(dst, dsem):
        pltpu.make_async_copy(ye_hbm.at[pl.ds(0, N_EXPERTS * COMBINE_WIN)], dst, dsem).wait()

    @pl.when(i == 0)
    def _():
        fetch(0, 0, buf.at[0], sem.at[0])

    @pl.when(i + 1 < nblocks)
    def _():
        fetch(i + 1, 0, buf.at[1 - slot], sem.at[1 - slot])

    v = grow_ref[...]
    hi = jnp.floor(v * (1.0 / 256.0))
    lo = v - hi * 256.0
    spread = spread_ref[...]
    row = _dot(hi.astype(BF16), spread) * 256.0 + _dot(lo.astype(BF16), spread) - 1.0
    width = N_EXPERTS * COMBINE_WIN
    lane = lax.broadcasted_iota(jnp.int32, (1, width), 1)
    lane_blk = lane >> (COMBINE_WIN.bit_length() - 1)
    lane_in = (lane & (COMBINE_WIN - 1)).astype(F32)

    def onehot(p):
        start = jnp.zeros((1, width), F32)
        lower = jnp.zeros((1, width), F32)
        for e in range(N_EXPERTS):
            start = jnp.where(lane_blk == e, win_start(i, e, p).astype(F32), start)
            lower = jnp.where(lane_blk == e,
                              (first_ref[i * N_EXPERTS + e] + p * COMBINE_WIN).astype(F32), lower)
        rel = row - start
        if p > 0:
            rel = jnp.where(row >= lower, rel, -1.0)
        return jnp.where(rel == lane_in, 1.0, 0.0).astype(BF16)

    wait(buf.at[slot], sem.at[slot])
    acc_ref[...] = _dot(onehot(0), buf[slot])
    for p in range(1, COMBINE_PASSES):
        @pl.when(p < npass_ref[i])
        def _():
            fetch(i, p, buf_extra, sem_extra.at[0])
            wait(buf_extra, sem_extra.at[0])
            acc_ref[...] += _dot(onehot(p), buf_extra[...])

    y = x1_ref[...] + acc_ref[...]
    if final_norm:
        y = _rms(y, fg_ref[...])
    out_ref[...] = y


def _combine(first, npass, grow, ye, x1, final_g, final_norm):
    n = x1.shape[0]
    width = N_EXPERTS * COMBINE_WIN
    lane = jnp.arange(width) // COMBINE_WIN
    spread = (jnp.arange(LANES)[:, None] == lane[None, :]).astype(BF16)
    tok = lambda i, f, p: (i, 0)
    full = lambda i, f, p: (0, 0)
    grid_spec = pltpu.PrefetchScalarGridSpec(
        num_scalar_prefetch=2,
        grid=(n // TOK_BLOCK,),
        in_specs=[pl.BlockSpec((TOK_BLOCK, LANES), tok),
                  pl.BlockSpec((LANES, width), full),
                  pl.BlockSpec(memory_space=pl.ANY),
                  pl.BlockSpec((TOK_BLOCK, D_MODEL), tok),
                  pl.BlockSpec((1, D_MODEL), full)],
        out_specs=pl.BlockSpec((TOK_BLOCK, D_MODEL), tok),
        scratch_shapes=[pltpu.VMEM((2, width, D_MODEL), BF16),
                        pltpu.VMEM((width, D_MODEL), BF16),
                        pltpu.VMEM((TOK_BLOCK, D_MODEL), F32),
                        pltpu.SemaphoreType.DMA((2,)),
                        pltpu.SemaphoreType.DMA((1,))],
    )
    kern = functools.partial(_combine_kernel, total_rows=ye.shape[0], final_norm=final_norm)
    return pl.pallas_call(
        kern,
        grid_spec=grid_spec,
        out_shape=jax.ShapeDtypeStruct((n, D_MODEL), F32),
        compiler_params=_cparams("arbitrary"),
        name="combine",
    )(first.reshape(-1), npass, grow, spread, ye, x1, final_g)


def _block_diag(w):
    g, d, _ = w.shape
    eye = jnp.eye(g, dtype=w.dtype)
    return (eye[:, None, :, None] * w[:, :, None, :]).reshape(g * d, g * d)


def _group_mean_matrix(channels, groups):
    gid = jnp.arange(channels) // (channels // groups)
    return jnp.where(gid[:, None] == gid[None, :], groups / channels, 0.0).astype(BF16)


def _prepare_layer(l, p):
    row = lambda a: a.reshape(1, -1)
    rw_t = p["router_w"][l].T
    rw_hi = rw_t.astype(BF16)
    rw_lo = (rw_t - rw_hi.astype(F32)).astype(BF16)
    lru = []
    for d in range(2):
        lru.append((p["lru_conv_w"][l], row(p["lru_conv_b"][l]),
                    _block_diag(p["lru_wa"][l, d]).astype(BF16), row(p["lru_ba"][l, d]),
                    _block_diag(p["lru_wx"][l, d]).astype(BF16), row(p["lru_bx"][l, d]),
                    row(p["lru_lambda"][l, d])))
    mix = (_block_diag(p["pool_w"][l]).astype(BF16), row(p["pool_scale"][l]),
           p["conv_dw_w"][l], row(p["conv_dw_b"][l]), row(p["conv_gn_g"][l]), row(p["conv_gn_b"][l]),
           _group_mean_matrix(MIX_CONV, CONV_GROUPS),
           p["conv_pw_w"][l].astype(BF16), row(p["conv_pw_b"][l]),
           p["w_out"][l].astype(BF16), row(p["norm2_g"][l]), rw_hi, rw_lo)
    return dict(norm1_g=row(p["norm1_g"][l]), w_in=p["w_in"][l].astype(BF16), lru=lru, mix=mix,
                wg=p["exp_w_gate"][l].astype(BF16), wu=p["exp_w_up"][l].astype(BF16),
                wd=p["exp_w_down"][l].astype(BF16))


def _layer(x, lw, final_g, *, seq_len, final_norm):
    n = x.shape[0]
    zp, zl, zg, zc = _in_proj(x, lw["norm1_g"], lw["w_in"])
    yf, conv = _lru_scan(zl, *lw["lru"][0], seq_len=seq_len, reverse=False)
    yb = _lru_scan(conv, *lw["lru"][1], seq_len=seq_len, reverse=True)
    x1, xn, aff = _mix_out(x, zp, zc, zg, yf, yb, lw["mix"], seq_len=seq_len)

    capacity = max(1, CAPACITY_FACTOR * n // N_EXPERTS)
    nb = n // TOK_BLOCK
    gpos, tot = _select(aff.reshape(N_EXPERTS, nb, TOK_BLOCK), capacity)

    cnt = tot[:, :, 0].astype(jnp.int32)
    base = jnp.cumsum(cnt, axis=1) - cnt
    nwin = jnp.where(cnt > 0, (base % SUBLANES + cnt + COMPACT_WIN - 1) // COMPACT_WIN, 0)
    cmp = _compact(base, nwin, gpos.reshape(N_EXPERTS, nb, 1, TOK_BLOCK), _payload(aff), capacity)
    idx = cmp[:, :capacity, LANE_BLOCK] * TOK_BLOCK + cmp[:, :capacity, LANE_LOCAL]
    idx = jnp.clip(idx.astype(jnp.int32), 0, n - 1).reshape(-1)
    ye = _ffn(idx, xn, cmp, lw["wg"], lw["wu"], lw["wd"], capacity)

    expert_row0 = jnp.arange(N_EXPERTS, dtype=jnp.int32)[:, None] * capacity
    gpos2 = gpos.reshape(N_EXPERTS, n)
    grow = jnp.where(gpos2 >= 0, gpos2 + expert_row0.astype(F32) + 1.0, 0.0).T
    grow = jnp.pad(grow, ((0, 0), (0, LANES - N_EXPERTS)))
    row0 = expert_row0 + base
    first = (row0 // BF16_ROWS) * BF16_ROWS
    passes = jnp.where(cnt > 0, (row0 % BF16_ROWS + cnt + COMBINE_WIN - 1) // COMBINE_WIN, 1)
    return _combine(first.T, jnp.max(passes, axis=0), grow, ye, x1, final_g, final_norm)


def _trunk(x, layers, final_g):
    b, s, d = x.shape
    h = x.reshape(b * s, d)
    for l, lw in enumerate(layers):
        h = _layer(h, lw, final_g, seq_len=s, final_norm=(l == len(layers) - 1))
    return h.reshape(b, s, d)


def kernel(x_prompt, x_sample, norm1_g, w_in, pool_w, pool_scale, lru_conv_w, lru_conv_b, lru_wa, lru_ba, lru_wx, lru_bx, lru_lambda, conv_dw_w, conv_dw_b, conv_gn_g, conv_gn_b, conv_pw_w, conv_pw_b, w_out, norm2_g, router_w, exp_w_gate, exp_w_up, exp_w_down, final_g):
    p = dict(norm1_g=norm1_g, w_in=w_in, pool_w=pool_w, pool_scale=pool_scale, lru_conv_w=lru_conv_w,
             lru_conv_b=lru_conv_b, lru_wa=lru_wa, lru_ba=lru_ba, lru_wx=lru_wx, lru_bx=lru_bx,
             lru_lambda=lru_lambda, conv_dw_w=conv_dw_w, conv_dw_b=conv_dw_b, conv_gn_g=conv_gn_g,
             conv_gn_b=conv_gn_b, conv_pw_w=conv_pw_w, conv_pw_b=conv_pw_b, w_out=w_out,
             norm2_g=norm2_g, router_w=router_w, exp_w_gate=exp_w_gate, exp_w_up=exp_w_up,
             exp_w_down=exp_w_down)
    layers = [_prepare_layer(l, p) for l in range(norm1_g.shape[0])]
    fg = final_g.reshape(1, -1)
    return (_trunk(x_prompt, layers, fg), _trunk(x_sample, layers, fg))
```

```python
import functools

import jax
import jax.numpy as jnp
from jax import lax
from jax.experimental import pallas as pl
from jax.experimental.pallas import tpu as pltpu

F32 = jnp.float32
BF16 = jnp.bfloat16

D_MODEL = 1024
MIX_POOL = 256
MIX_LRU = 512
MIX_CONV = 256
IN_COLS = MIX_POOL + 2 * MIX_LRU + 2 * MIX_CONV
POOL_GROUP_WIDTH = 64
POOL_MAX_HALF = 8
LRU_CONV_WIDTH = 4
RG_C = 8.0
CONV_WIDTH = 31
CONV_GROUPS = 4
N_EXPERTS = 16
CAPACITY_FACTOR = 2
D_FF = 11 * D_MODEL // 8
RMS_EPS = 1e-6
GN_EPS = 1e-5

SUBLANES = 8
ROW_TILE = 512
SCAN_TILE = 512
HALO = 16
TOK_BLOCK = 256
LANES = 128
BF16_ROWS = 16
COMPACT_WIN = 72
COMPACT_UNROLL = 4
FFN_ROWS = 1024
FFN_SUB = 512
COMBINE_WIN = 64
LANE_LOCAL = 64
VMEM_LIMIT = 48 * 1024 * 1024


def _cparams(*sem):
    return pltpu.CompilerParams(dimension_semantics=sem, vmem_limit_bytes=VMEM_LIMIT)


def _rms(x, g):
    return x * lax.rsqrt(jnp.mean(x * x, axis=-1, keepdims=True) + RMS_EPS) * g


def _split_bf16(x):
    hi = x.astype(BF16)
    lo = (x - hi.astype(F32)).astype(BF16)
    return hi, lo


def _dot(a, b):
    return jnp.dot(a, b, preferred_element_type=F32)


def _in_proj_kernel(x_ref, g_ref, w_ref, zp_ref, zl_ref, zg_ref, zc_ref):
    h = _rms(x_ref[...], g_ref[...])
    z = _dot(h.astype(BF16), w_ref[...])
    o1 = MIX_POOL
    o2 = o1 + MIX_LRU
    o3 = o2 + MIX_LRU
    zp_ref[...] = z[:, :o1]
    zl_ref[...] = z[:, o1:o2]
    zg_ref[...] = z[:, o2:o3]
    zc_ref[...] = z[:, o3:]


def _in_proj(x, g, w_bf16):
    n = x.shape[0]
    tm = min(ROW_TILE, n)
    row = lambda i: (i, 0)
    full = lambda i: (0, 0)
    widths = (MIX_POOL, MIX_LRU, MIX_LRU, 2 * MIX_CONV)
    return pl.pallas_call(
        _in_proj_kernel,
        grid=(n // tm,),
        in_specs=[pl.BlockSpec((tm, D_MODEL), row),
                  pl.BlockSpec((1, D_MODEL), full),
                  pl.BlockSpec((D_MODEL, IN_COLS), full)],
        out_specs=[pl.BlockSpec((tm, w), row) for w in widths],
        out_shape=[jax.ShapeDtypeStruct((n, w), F32) for w in widths],
        compiler_params=_cparams("parallel"),
        name="in_proj",
    )(x, g, w_bf16)


def _shift_rows(x, m):
    return x if m == 0 else pltpu.roll(x, x.shape[0] - m, axis=0)


def _lru_scan_kernel(*refs, tiles_per_seq, reverse):
    if reverse:
        c_ref, wa_ref, ba_ref, wx_ref, bx_ref, lam_ref, y_ref, a_ref, b_ref, h_ref = refs
    else:
        (u_ref, prev_ref, next_ref, cw_ref, cb_ref, wa_ref, ba_ref, wx_ref, bx_ref, lam_ref,
         y_ref, c_ref, a_ref, b_ref, h_ref) = refs
    t = y_ref.shape[0]
    j = pl.program_id(1)

    @pl.when(j == 0)
    def _():
        h_ref[...] = jnp.zeros_like(h_ref)

    if reverse:
        c = c_ref[...]
    else:
        ext = jnp.concatenate([jnp.where(j > 0, prev_ref[...], 0.0), u_ref[...],
                               jnp.where(j < tiles_per_seq - 1, next_ref[...], 0.0)], axis=0)
        pad_lo = LRU_CONV_WIDTH // 2
        c = cb_ref[...]
        for k in range(LRU_CONV_WIDTH):
            c = c + cw_ref[k:k + 1, :] * _shift_rows(ext, SUBLANES - pad_lo + k)[0:t, :]
        c_ref[...] = c
    cb16 = c.astype(BF16)
    r = jax.nn.sigmoid(_dot(cb16, wa_ref[...]) + ba_ref[...])
    i = jax.nn.sigmoid(_dot(cb16, wx_ref[...]) + bx_ref[...])
    neg_lam = -lam_ref[...]
    softplus = jnp.maximum(neg_lam, 0.0) + jnp.log1p(jnp.exp(-jnp.abs(neg_lam)))
    log_a = -RG_C * r * softplus
    th = jnp.tanh(log_a)
    one_minus_a2 = -2.0 * th / (1.0 - th)
    a_ref[...] = jnp.exp(log_a)
    b_ref[...] = jnp.sqrt(one_minus_a2) * (i * c)

    groups = t // SUBLANES
    sub = lax.broadcasted_iota(jnp.int32, (SUBLANES, MIX_LRU), 0)

    def body(g, carry):
        gi = groups - 1 - g if reverse else g
        start = pl.multiple_of(gi * SUBLANES, SUBLANES)
        a8 = a_ref[pl.ds(start, SUBLANES), :]
        b8 = b_ref[pl.ds(start, SUBLANES), :]
        for k in (1, 2, 4):
            shift = SUBLANES - k if reverse else k
            m = (sub < SUBLANES - k) if reverse else (sub >= k)
            a_sh = pltpu.roll(a8, shift, axis=0)
            b_sh = pltpu.roll(b8, shift, axis=0)
            b8 = jnp.where(m, a8 * b_sh + b8, b8)
            a8 = jnp.where(m, a8 * a_sh, a8)
        h = a8 * carry + b8
        y_ref[pl.ds(start, SUBLANES), :] = h
        edge = h[0:1, :] if reverse else h[SUBLANES - 1:SUBLANES, :]
        return jnp.broadcast_to(edge, (SUBLANES, MIX_LRU))

    h_ref[...] = lax.fori_loop(0, groups, body, h_ref[...], unroll=8)


def _lru_scan(src, cw, cb, wa_bd, ba, wx_bd, bx, lam, *, seq_len, reverse):
    n = src.shape[0]
    t = min(SCAN_TILE, seq_len)
    nt = seq_len // t
    nseq = n // seq_len
    halo_blocks = n // SUBLANES
    tile_blocks = t // SUBLANES

    def tile_of(s, j):
        return s * nt + ((nt - 1 - j) if reverse else j)

    def main_map(s, j):
        return (tile_of(s, j), 0)

    def prev_map(s, j):
        return (jnp.maximum(tile_of(s, j) * tile_blocks - 1, 0), 0)

    def next_map(s, j):
        return (jnp.minimum((tile_of(s, j) + 1) * tile_blocks, halo_blocks - 1), 0)

    full = lambda s, j: (0, 0)
    vec = pl.BlockSpec((1, MIX_LRU), full)
    mat = pl.BlockSpec((MIX_LRU, MIX_LRU), full)
    tile = pl.BlockSpec((t, MIX_LRU), main_map)
    out = jax.ShapeDtypeStruct((n, MIX_LRU), F32)
    gate_specs = [mat, vec, mat, vec, vec]
    if reverse:
        in_specs, args = [tile] + gate_specs, (src, wa_bd, ba, wx_bd, bx, lam)
        out_specs, out_shape = tile, out
    else:
        in_specs = [tile, pl.BlockSpec((SUBLANES, MIX_LRU), prev_map),
                    pl.BlockSpec((SUBLANES, MIX_LRU), next_map),
                    pl.BlockSpec((LRU_CONV_WIDTH, MIX_LRU), full), vec] + gate_specs
        args = (src, src, src, cw, cb, wa_bd, ba, wx_bd, bx, lam)
        out_specs, out_shape = [tile, tile], [out, out]
    return pl.pallas_call(
        functools.partial(_lru_scan_kernel, tiles_per_seq=nt, reverse=reverse),
        grid=(nseq, nt),
        in_specs=in_specs,
        out_specs=out_specs,
        out_shape=out_shape,
        scratch_shapes=[pltpu.VMEM((t, MIX_LRU), F32),
                        pltpu.VMEM((t, MIX_LRU), F32),
                        pltpu.VMEM((SUBLANES, MIX_LRU), F32)],
        compiler_params=_cparams("arbitrary", "arbitrary"),
        name="lru_scan_bwd" if reverse else "lru_scan_fwd",
    )(*args)


def _gelu_tanh(x):
    return 0.5 * x * (1.0 + jnp.tanh(0.7978845608028654 * (x + 0.044715 * (x * x * x))))


def _mix_out_kernel(x_ref, zp_ref, zp_prev_ref, zp_next_ref, zc_ref, zc_prev_ref, zc_next_ref,
                    zg_ref, yf_ref, yb_ref,
                    pool_w_ref, pool_s_ref, dw_w_ref, dw_b_ref, gn_g_ref, gn_b_ref, gavg_ref,
                    pw_w_ref, pw_b_ref, wout_ref, g2_ref, rw_hi_ref, rw_lo_ref,
                    x1_ref, xn_ref, aff_ref, payload_ref, *, tiles_per_seq, seq_len):
    t = x_ref.shape[0]
    jj = pl.program_id(0) % tiles_per_seq
    has_prev = jj > 0
    has_next = jj < tiles_per_seq - 1

    pext = jnp.concatenate([jnp.where(has_prev, zp_prev_ref[...], 0.0), zp_ref[...],
                            jnp.where(has_next, zp_next_ref[...], 0.0)], axis=0)
    lane = lax.broadcasted_iota(jnp.int32, (1, MIX_POOL), 1)
    half = jnp.left_shift(1, lane // POOL_GROUP_WIDTH)
    acc = jnp.zeros((t, MIX_POOL), F32)
    for m in range(SUBLANES):
        shifted = _shift_rows(pext, m)
        for off in range(HALO - POOL_MAX_HALF + m, HALO + POOL_MAX_HALF, SUBLANES):
            d = off - HALO
            in_window = jnp.where((d >= -half) & (d < half), 1.0, 0.0)
            acc = acc + in_window * shifted[off - m:off - m + t, :]
    pos = jj * t + lax.broadcasted_iota(jnp.int32, (t, MIX_POOL), 0)
    count = jnp.minimum(pos + half, seq_len) - jnp.maximum(pos - half, 0)
    dmean = acc / count.astype(F32) - zp_ref[...]
    y_pool = _dot(dmean.astype(BF16), pool_w_ref[...]) * pool_s_ref[...]

    def glu(z):
        return z[:, :MIX_CONV] * jax.nn.sigmoid(z[:, MIX_CONV:])

    vext = jnp.concatenate([glu(jnp.where(has_prev, zc_prev_ref[...], 0.0)), glu(zc_ref[...]),
                            glu(jnp.where(has_next, zc_next_ref[...], 0.0))], axis=0)
    v = jnp.zeros((t, MIX_CONV), F32) + dw_b_ref[...]
    first_off = HALO - CONV_WIDTH // 2
    for m in range(SUBLANES):
        shifted = _shift_rows(vext, m)
        for off in range(first_off + (m - first_off) % SUBLANES, first_off + CONV_WIDTH, SUBLANES):
            k = off - first_off
            v = v + dw_w_ref[k:k + 1, :] * shifted[off - m:off - m + t, :]
    gavg = gavg_ref[...]
    v_hi, v_lo = _split_bf16(v)
    mu = _dot(v_hi, gavg) + _dot(v_lo, gavg)
    dv = v - mu
    sq_hi, sq_lo = _split_bf16(dv * dv)
    var = _dot(sq_hi, gavg) + _dot(sq_lo, gavg)
    yn = dv * lax.rsqrt(var + GN_EPS) * gn_g_ref[...] + gn_b_ref[...]
    sw = yn * jax.nn.sigmoid(yn)
    y_conv = _dot(sw.astype(BF16), pw_w_ref[...]) + pw_b_ref[...]

    y_lru = (yf_ref[...] + yb_ref[...]) * _gelu_tanh(zg_ref[...])

    o1 = MIX_POOL
    o2 = o1 + MIX_LRU
    mixed = (_dot(y_pool.astype(BF16), wout_ref[0:o1, :])
             + _dot(y_lru.astype(BF16), wout_ref[o1:o2, :])
             + _dot(y_conv.astype(BF16), wout_ref[o2:, :]))
    x1 = x_ref[...] + mixed
    x1_ref[...] = x1

    xn = _rms(x1, g2_ref[...])
    xn_ref[...] = xn
    xn_hi, xn_lo = _split_bf16(xn)
    nt_dims = (((1,), (1,)), ((), ()))
    logits = (lax.dot_general(rw_hi_ref[...], xn_hi, nt_dims, preferred_element_type=F32)
              + lax.dot_general(rw_hi_ref[...], xn_lo, nt_dims, preferred_element_type=F32)
              + lax.dot_general(rw_lo_ref[...], xn_hi, nt_dims, preferred_element_type=F32))
    e = jnp.exp(logits - jnp.max(logits, axis=0, keepdims=True))
    aff = e / jnp.sum(e, axis=0, keepdims=True)
    aff_ref[...] = aff

    p0 = aff.astype(BF16).astype(F32)
    r1 = aff - p0
    p1 = r1.astype(BF16).astype(F32)
    p2 = r1 - p1
    tok = pl.program_id(0) * t + lax.broadcasted_iota(jnp.int32, (SUBLANES, t), 1)
    which = lax.broadcasted_iota(jnp.int32, (SUBLANES, t), 0)
    block = jnp.right_shift(tok, TOK_BLOCK.bit_length() - 1)
    meta = jnp.where(which == 0, tok & (TOK_BLOCK - 1), jnp.where(which == 1, block, 0))
    rows = [p0, p1, p2, jnp.zeros((LANE_LOCAL - 3 * N_EXPERTS, t), F32), meta.astype(F32),
            jnp.zeros((LANES - LANE_LOCAL - SUBLANES, t), F32)]
    payload_ref[...] = jnp.concatenate(rows, axis=0).T.astype(BF16)


def _mix_out(x, zp, zc, zg, yf, yb, wts, *, seq_len):
    n = x.shape[0]
    t = min(ROW_TILE, seq_len)
    nt = seq_len // t
    halo_blocks = n // HALO
    tile_blocks = t // HALO
    row = lambda i: (i, 0)
    prev = lambda i: (jnp.maximum(i * tile_blocks - 1, 0), 0)
    nxt = lambda i: (jnp.minimum((i + 1) * tile_blocks, halo_blocks - 1), 0)
    full = lambda i: (0, 0)

    def wspec(a):
        return pl.BlockSpec(a.shape, full)

    kern = functools.partial(_mix_out_kernel, tiles_per_seq=nt, seq_len=seq_len)
    return pl.pallas_call(
        kern,
        grid=(n // t,),
        in_specs=[pl.BlockSpec((t, D_MODEL), row),
                  pl.BlockSpec((t, MIX_POOL), row),
                  pl.BlockSpec((HALO, MIX_POOL), prev),
                  pl.BlockSpec((HALO, MIX_POOL), nxt),
                  pl.BlockSpec((t, 2 * MIX_CONV), row),
                  pl.BlockSpec((HALO, 2 * MIX_CONV), prev),
                  pl.BlockSpec((HALO, 2 * MIX_CONV), nxt),
                  pl.BlockSpec((t, MIX_LRU), row),
                  pl.BlockSpec((t, MIX_LRU), row),
                  pl.BlockSpec((t, MIX_LRU), row)] + [wspec(a) for a in wts],
        out_specs=[pl.BlockSpec((t, D_MODEL), row),
                   pl.BlockSpec((t, D_MODEL), row),
                   pl.BlockSpec((N_EXPERTS, t), lambda i: (0, i)),
                   pl.BlockSpec((t, LANES), row)],
        out_shape=[jax.ShapeDtypeStruct((n, D_MODEL), F32),
                   jax.ShapeDtypeStruct((n, D_MODEL), F32),
                   jax.ShapeDtypeStruct((N_EXPERTS, n), F32),
                   jax.ShapeDtypeStruct((n, LANES), BF16)],
        compiler_params=_cparams("parallel"),
        name="mix_out",
    )(x, zp, zp, zp, zc, zc, zc, zg, yf, yb, *wts)


def _select_kernel(aff_ref, gpos_ref, tot_ref, *, capacity, n_tokens):
    n_exp, nb = aff_ref.shape[0], aff_ref.shape[1]
    cap = jnp.float32(capacity)
    experts = range(n_exp)

    def count(mask):
        return jnp.sum(jnp.where(mask, 1.0, 0.0), keepdims=True)

    def as_f32(bits):
        return lax.bitcast_convert_type(bits, F32)

    def value_step(it, bits):
        bit = jnp.left_shift(1, 30 - it)
        out = []
        for e in experts:
            cand = bits[e] | bit
            out.append(jnp.where(count(aff_ref[e] >= as_f32(cand)) >= cap, cand, bits[e]))
        return tuple(out)

    zero = jnp.zeros((1, 1), jnp.int32)
    bits = lax.fori_loop(0, 31, value_step, (zero,) * n_exp)

    def masks(e):
        v = aff_ref[e]
        above = v >= as_f32(bits[e] + 1)
        return above, (v >= as_f32(bits[e])) & jnp.logical_not(above)

    need = [cap - count(masks(e)[0]) for e in experts]
    idx = (lax.broadcasted_iota(jnp.int32, (nb, TOK_BLOCK), 0) * TOK_BLOCK
           + lax.broadcasted_iota(jnp.int32, (nb, TOK_BLOCK), 1))
    index_bits = (n_tokens - 1).bit_length()

    def index_step(it, last):
        bit = jnp.left_shift(1, index_bits - 1 - it)
        out = []
        for e in experts:
            cand = last[e] | bit
            out.append(jnp.where(count(masks(e)[1] & (idx < cand)) < need[e], cand, last[e]))
        return tuple(out)

    last = lax.fori_loop(0, index_bits, index_step, (zero,) * n_exp)

    r = lax.broadcasted_iota(jnp.int32, (TOK_BLOCK, TOK_BLOCK), 0)
    c = lax.broadcasted_iota(jnp.int32, (TOK_BLOCK, TOK_BLOCK), 1)
    before = jnp.where(r < c, 1.0, 0.0).astype(BF16)
    ones = jnp.ones((TOK_BLOCK, TOK_BLOCK), BF16)
    rb = lax.broadcasted_iota(jnp.int32, (nb, nb), 0)
    cb = lax.broadcasted_iota(jnp.int32, (nb, nb), 1)
    earlier = jnp.where(cb < rb, 1.0, 0.0).astype(BF16)
    for e in experts:
        above, tie = masks(e)
        sel = above | (tie & (idx <= last[e]))
        sel16 = jnp.where(sel, 1.0, 0.0).astype(BF16)
        within = _dot(sel16, before)
        tot = _dot(sel16, ones)
        base = _dot(earlier, tot.astype(BF16))
        gpos_ref[e] = jnp.where(sel, base + within, -1.0)
        tot_ref[e] = tot[:, :LANES]


def _select(aff3, capacity):
    e, nb, _ = aff3.shape
    kern = functools.partial(_select_kernel, capacity=capacity, n_tokens=nb * TOK_BLOCK)
    whole = lambda i: (0, 0, 0)
    return pl.pallas_call(
        kern,
        grid=(1,),
        in_specs=[pl.BlockSpec((e, nb, TOK_BLOCK), whole)],
        out_specs=[pl.BlockSpec((e, nb, TOK_BLOCK), whole),
                   pl.BlockSpec((e, nb, LANES), whole)],
        out_shape=[jax.ShapeDtypeStruct((e, nb, TOK_BLOCK), F32),
                   jax.ShapeDtypeStruct((e, nb, LANES), F32)],
        compiler_params=_cparams("arbitrary"),
        name="select",
    )(aff3)


def _compact_kernel(base_ref, nwin_ref, gpos_ref, payload_ref, out_ref, tokrow_ref):
    e = pl.program_id(0)
    nb = gpos_ref.shape[0]
    out_ref[...] = jnp.zeros_like(out_ref)
    slot = lax.broadcasted_iota(jnp.int32, (COMPACT_WIN, TOK_BLOCK), 0).astype(F32)

    def moved(b, w):
        start = pl.multiple_of((base_ref[e * nb + b] // SUBLANES) * SUBLANES + w * COMPACT_WIN, SUBLANES)
        rows = payload_ref[pl.ds(pl.multiple_of(b * TOK_BLOCK, TOK_BLOCK), TOK_BLOCK), :]
        onehot = jnp.where(slot == gpos_ref[b] - start.astype(F32), 1.0, 0.0).astype(BF16)
        return start, _dot(onehot, rows)

    def group(g, carry):
        blocks = [g * COMPACT_UNROLL + u for u in range(COMPACT_UNROLL)]
        for start, rows in [moved(b, 0) for b in blocks]:
            out_ref[pl.ds(start, COMPACT_WIN), :] += rows

        for b in blocks:
            def window(w, c, b=b):
                start, rows = moved(b, w)
                out_ref[pl.ds(start, COMPACT_WIN), :] += rows
                return c
            lax.fori_loop(1, nwin_ref[e * nb + b], window, 0)
        return carry

    lax.fori_loop(0, nb // COMPACT_UNROLL, group, 0)

    capacity = tokrow_ref.shape[1]
    pick = lax.broadcasted_iota(jnp.int32, (SUBLANES, LANES), 1) - LANE_LOCAL
    pick = jnp.where(pick == lax.broadcasted_iota(jnp.int32, (SUBLANES, LANES), 0), 1.0, 0.0)
    tokrow_ref[...] = lax.dot_general(pick.astype(BF16), out_ref[0:capacity, :].astype(BF16),
                                      (((1,), (1,)), ((), ())), preferred_element_type=F32)


def _compact(base, nwin, gpos4, payload, capacity):
    n_exp, nb = gpos4.shape[0], gpos4.shape[1]
    n = payload.shape[0]
    rows = capacity + COMPACT_WIN
    grid_spec = pltpu.PrefetchScalarGridSpec(
        num_scalar_prefetch=2,
        grid=(n_exp,),
        in_specs=[pl.BlockSpec((None, nb, 1, TOK_BLOCK), lambda e, b, w: (e, 0, 0, 0)),
                  pl.BlockSpec((n, LANES), lambda e, b, w: (0, 0))],
        out_specs=[pl.BlockSpec((None, rows, LANES), lambda e, b, w: (e, 0, 0)),
                   pl.BlockSpec((None, SUBLANES, capacity), lambda e, b, w: (e, 0, 0))],
    )
    return pl.pallas_call(
        _compact_kernel,
        grid_spec=grid_spec,
        out_shape=[jax.ShapeDtypeStruct((n_exp, rows, LANES), F32),
                   jax.ShapeDtypeStruct((n_exp, SUBLANES, capacity), F32)],
        compiler_params=_cparams("arbitrary"),
        name="compact",
    )(base.reshape(-1), nwin.reshape(-1), gpos4, payload)


def _ffn_kernel(idx_hbm, xn_hbm, cmp_ref, wg_ref, wu_ref, wd_ref, ye_ref,
                idx_smem, xbuf, idx_sem, row_sem, *, steps_per_expert, tf):
    e = pl.program_id(0)
    s = e * steps_per_expert + pl.program_id(1)
    total = pl.num_programs(0) * steps_per_expert
    slot = s % 2

    def aligned(v):
        return v if isinstance(v, int) else pl.multiple_of(v, tf)

    def idx_copy(chunk, sl):
        return pltpu.make_async_copy(idx_hbm.at[pl.ds(aligned(chunk * tf), tf)],
                                     idx_smem.at[pl.ds(aligned(sl * tf), tf)], idx_sem.at[sl])

    def start_row(sl, r):
        tok = idx_smem[sl * tf + r]
        pltpu.make_async_copy(xn_hbm.at[pl.ds(tok, 1)], xbuf.at[sl, pl.ds(r, 1)],
                              row_sem.at[sl]).start()

    def wait_rows(sl):
        pltpu.make_async_copy(xn_hbm.at[pl.ds(0, tf)], xbuf.at[sl], row_sem.at[sl]).wait()

    @pl.when(s == 0)
    def _():
        idx_copy(0, 0).start()
        idx_copy(0, 0).wait()

        def issue(r, c):
            start_row(0, r)
            return c
        lax.fori_loop(0, tf, issue, 0, unroll=8)
        idx_copy(1, 1).start()

    @pl.when(s + 1 < total)
    def _():
        idx_copy(s + 1, 1 - slot).wait()

    wait_rows(slot)

    lane = lax.broadcasted_iota(jnp.int32, (tf, LANES), 1)
    gate_lanes = ((lane & (N_EXPERTS - 1)) == e) & (lane < 3 * N_EXPERTS)
    gate = jnp.sum(jnp.where(gate_lanes, cmp_ref[...], 0.0), axis=1, keepdims=True)

    sub = min(FFN_SUB, tf)
    pieces = max(1, 3 * (tf // sub) - 2)
    issued = 0

    def issue_piece(p):
        nonlocal issued
        upto = min(tf, tf * (p + 1) // pieces)
        for r in range(issued, upto):
            start_row(1 - slot, r)
        issued = upto

    for hi, h0 in enumerate(range(0, tf, sub)):
        xe = xbuf[slot, h0:h0 + sub, :].astype(BF16)
        g = _dot(xe, wg_ref[...])
        issue_piece(3 * hi)
        u = _dot(xe, wu_ref[...])
        issue_piece(3 * hi + 1)
        h = (g * jax.nn.sigmoid(g)) * u
        y = _dot(h.astype(BF16), wd_ref[...]) * gate[h0:h0 + sub, :]
        ye_ref[h0:h0 + sub, :] = y.astype(BF16)
        issue_piece(3 * hi + 2)

    @pl.when(s + 2 < total)
    def _():
        idx_copy(s + 2, slot).start()

    @pl.when(s == total - 1)
    def _():
        wait_rows(1 - slot)


def _ffn(idx, xn, cmp, wg, wu, wd, capacity):
    n_exp = wg.shape[0]
    tf = min(FFN_ROWS, capacity)
    per = capacity // tf
    kern = functools.partial(_ffn_kernel, steps_per_expert=per, tf=tf)
    return pl.pallas_call(
        kern,
        grid=(n_exp, per),
        in_specs=[pl.BlockSpec(memory_space=pl.ANY),
                  pl.BlockSpec(memory_space=pl.ANY),
                  pl.BlockSpec((None, tf, LANES), lambda e, i: (e, i, 0)),
                  pl.BlockSpec((None, D_MODEL, D_FF), lambda e, i: (e, 0, 0)),
                  pl.BlockSpec((None, D_MODEL, D_FF), lambda e, i: (e, 0, 0)),
                  pl.BlockSpec((None, D_FF, D_MODEL), lambda e, i: (e, 0, 0))],
        out_specs=pl.BlockSpec((tf, D_MODEL), lambda e, i: (e * per + i, 0)),
        out_shape=jax.ShapeDtypeStruct((n_exp * capacity, D_MODEL), BF16),
        scratch_shapes=[pltpu.SMEM((2 * tf,), jnp.int32),
                        pltpu.VMEM((2, tf, D_MODEL), F32),
                        pltpu.SemaphoreType.DMA((2,)),
                        pltpu.SemaphoreType.DMA((2,))],
        compiler_params=_cparams("arbitrary", "arbitrary"),
        name="ffn",
    )(idx, xn, cmp, wg, wu, wd)


def _combine_kernel(first_ref, npass_ref, grow_ref, spread_ref, ye_hbm, x1_ref, fg_ref, out_ref,
                    buf, buf_extra, acc_ref, sem, sem_extra, *, total_rows, final_norm):
    i = pl.program_id(0)
    nblocks = pl.num_programs(0)
    slot = i % 2
    last_start = total_rows - COMBINE_WIN

    def win_start(blk, e, p):
        return pl.multiple_of(jnp.minimum(first_ref[blk * N_EXPERTS + e] + p * COMBINE_WIN, last_start),
                              BF16_ROWS)

    def fetch(blk, p, dst, dsem):
        for e in range(N_EXPERTS):
            pltpu.make_async_copy(ye_hbm.at[pl.ds(win_start(blk, e, p), COMBINE_WIN)],
                                  dst.at[pl.ds(e * COMBINE_WIN, COMBINE_WIN)], dsem).start()

    def wait(dst, dsem):
        pltpu.make_async_copy(ye_hbm.at[pl.ds(0, N_EXPERTS * COMBINE_WIN)], dst, dsem).wait()

    @pl.when(i == 0)
    def _():
        fetch(0, 0, buf.at[0], sem.at[0])

    @pl.when(i + 1 < nblocks)
    def _():
        fetch(i + 1, 0, buf.at[1 - slot], sem.at[1 - slot])

    v = grow_ref[...]
    hi = jnp.floor(v * (1.0 / 256.0))
    lo = v - hi * 256.0
    spread = spread_ref[...]
    row = _dot(hi.astype(BF16), spread) * 256.0 + _dot(lo.astype(BF16), spread) - 1.0
    width = N_EXPERTS * COMBINE_WIN
    lane = lax.broadcasted_iota(jnp.int32, (1, width), 1)
    lane_blk = lane >> (COMBINE_WIN.bit_length() - 1)
    lane_in = (lane & (COMBINE_WIN - 1)).astype(F32)

    def onehot(p):
        start = jnp.zeros((1, width), F32)
        lower = jnp.zeros((1, width), F32)
        for e in range(N_EXPERTS):
            start = jnp.where(lane_blk == e, win_start(i, e, p).astype(F32), start)
            lower = jnp.where(lane_blk == e,
                              (first_ref[i * N_EXPERTS + e] + p * COMBINE_WIN).astype(F32), lower)
        rel = row - start
        if not isinstance(p, int) or p > 0:
            rel = jnp.where(row >= lower, rel, -1.0)
        return jnp.where(rel == lane_in, 1.0, 0.0).astype(BF16)

    wait(buf.at[slot], sem.at[slot])
    acc_ref[...] = _dot(onehot(0), buf[slot])

    def extra_pass(p, carry):
        fetch(i, p, buf_extra, sem_extra.at[0])
        wait(buf_extra, sem_extra.at[0])
        acc_ref[...] += _dot(onehot(p), buf_extra[...])
        return carry

    lax.fori_loop(1, npass_ref[i], extra_pass, 0)

    y = x1_ref[...] + acc_ref[...]
    if final_norm:
        y = _rms(y, fg_ref[...])
    out_ref[...] = y


def _combine(first, npass, grow, ye, x1, final_g, final_norm):
    n = x1.shape[0]
    width = N_EXPERTS * COMBINE_WIN
    lane = jnp.arange(width) // COMBINE_WIN
    spread = (jnp.arange(LANES)[:, None] == lane[None, :]).astype(BF16)
    tok = lambda i, f, p: (i, 0)
    full = lambda i, f, p: (0, 0)
    grid_spec = pltpu.PrefetchScalarGridSpec(
        num_scalar_prefetch=2,
        grid=(n // TOK_BLOCK,),
        in_specs=[pl.BlockSpec((TOK_BLOCK, LANES), tok),
                  pl.BlockSpec((LANES, width), full),
                  pl.BlockSpec(memory_space=pl.ANY),
                  pl.BlockSpec((TOK_BLOCK, D_MODEL), tok),
                  pl.BlockSpec((1, D_MODEL), full)],
        out_specs=pl.BlockSpec((TOK_BLOCK, D_MODEL), tok),
        scratch_shapes=[pltpu.VMEM((2, width, D_MODEL), BF16),
                        pltpu.VMEM((width, D_MODEL), BF16),
                        pltpu.VMEM((TOK_BLOCK, D_MODEL), F32),
                        pltpu.SemaphoreType.DMA((2,)),
                        pltpu.SemaphoreType.DMA((1,))],
    )
    kern = functools.partial(_combine_kernel, total_rows=ye.shape[0], final_norm=final_norm)
    return pl.pallas_call(
        kern,
        grid_spec=grid_spec,
        out_shape=jax.ShapeDtypeStruct((n, D_MODEL), F32),
        compiler_params=_cparams("arbitrary"),
        name="combine",
    )(first.reshape(-1), npass, grow, spread, ye, x1, final_g)


def _block_diag(w):
    g, d, _ = w.shape
    eye = jnp.eye(g, dtype=w.dtype)
    return (eye[:, None, :, None] * w[:, :, None, :]).reshape(g * d, g * d)


def _group_mean_matrix(channels, groups):
    gid = jnp.arange(channels) // (channels // groups)
    return jnp.where(gid[:, None] == gid[None, :], groups / channels, 0.0).astype(BF16)


def _prepare_layer(l, p):
    row = lambda a: a.reshape(1, -1)
    rw_t = p["router_w"][l].T
    rw_hi = rw_t.astype(BF16)
    rw_lo = (rw_t - rw_hi.astype(F32)).astype(BF16)
    lru = []
    for d in range(2):
        lru.append((p["lru_conv_w"][l], row(p["lru_conv_b"][l]),
                    _block_diag(p["lru_wa"][l, d]).astype(BF16), row(p["lru_ba"][l, d]),
                    _block_diag(p["lru_wx"][l, d]).astype(BF16), row(p["lru_bx"][l, d]),
                    row(p["lru_lambda"][l, d])))
    mix = (_block_diag(p["pool_w"][l]).astype(BF16), row(p["pool_scale"][l]),
           p["conv_dw_w"][l], row(p["conv_dw_b"][l]), row(p["conv_gn_g"][l]), row(p["conv_gn_b"][l]),
           _group_mean_matrix(MIX_CONV, CONV_GROUPS),
           p["conv_pw_w"][l].astype(BF16), row(p["conv_pw_b"][l]),
           p["w_out"][l].astype(BF16), row(p["norm2_g"][l]), rw_hi, rw_lo)
    return dict(norm1_g=row(p["norm1_g"][l]), w_in=p["w_in"][l].astype(BF16), lru=lru, mix=mix,
                wg=p["exp_w_gate"][l].astype(BF16), wu=p["exp_w_up"][l].astype(BF16),
                wd=p["exp_w_down"][l].astype(BF16))


def _layer(x, lw, final_g, *, seq_len, final_norm):
    n = x.shape[0]
    zp, zl, zg, zc = _in_proj(x, lw["norm1_g"], lw["w_in"])
    yf, conv = _lru_scan(zl, *lw["lru"][0], seq_len=seq_len, reverse=False)
    yb = _lru_scan(conv, *lw["lru"][1], seq_len=seq_len, reverse=True)
    x1, xn, aff, payload = _mix_out(x, zp, zc, zg, yf, yb, lw["mix"], seq_len=seq_len)

    capacity = max(1, CAPACITY_FACTOR * n // N_EXPERTS)
    nb = n // TOK_BLOCK
    gpos, tot = _select(aff.reshape(N_EXPERTS, nb, TOK_BLOCK), capacity)

    cnt = tot[:, :, 0].astype(jnp.int32)
    base = jnp.cumsum(cnt, axis=1) - cnt
    nwin = jnp.where(cnt > 0, (base % SUBLANES + cnt + COMPACT_WIN - 1) // COMPACT_WIN, 0)
    cmp, tokrow = _compact(base, nwin, gpos.reshape(N_EXPERTS, nb, 1, TOK_BLOCK), payload, capacity)
    idx = tokrow[:, 1, :] * TOK_BLOCK + tokrow[:, 0, :]
    idx = jnp.clip(idx.astype(jnp.int32), 0, n - 1).reshape(-1)
    ye = _ffn(idx, xn, cmp, lw["wg"], lw["wu"], lw["wd"], capacity)

    expert_row0 = jnp.arange(N_EXPERTS, dtype=jnp.int32)[:, None] * capacity
    gpos2 = gpos.reshape(N_EXPERTS, n)
    grow = jnp.where(gpos2 >= 0, gpos2 + expert_row0.astype(F32) + 1.0, 0.0).T
    grow = jnp.pad(grow, ((0, 0), (0, LANES - N_EXPERTS)))
    row0 = expert_row0 + base
    first = (row0 // BF16_ROWS) * BF16_ROWS
    passes = jnp.where(cnt > 0, (row0 % BF16_ROWS + cnt + COMBINE_WIN - 1) // COMBINE_WIN, 1)
    return _combine(first.T, jnp.max(passes, axis=0), grow, ye, x1, final_g, final_norm)


def _trunk(x, layers, final_g):
    b, s, d = x.shape
    h = x.reshape(b * s, d)
    for l, lw in enumerate(layers):
        h = _layer(h, lw, final_g, seq_len=s, final_norm=(l == len(layers) - 1))
    return h.reshape(b, s, d)


def kernel(x_prompt, x_sample, norm1_g, w_in, pool_w, pool_scale, lru_conv_w, lru_conv_b, lru_wa, lru_ba, lru_wx, lru_bx, lru_lambda, conv_dw_w, conv_dw_b, conv_gn_g, conv_gn_b, conv_pw_w, conv_pw_b, w_out, norm2_g, router_w, exp_w_gate, exp_w_up, exp_w_down, final_g):
    p = dict(norm1_g=norm1_g, w_in=w_in, pool_w=pool_w, pool_scale=pool_scale, lru_conv_w=lru_conv_w,
             lru_conv_b=lru_conv_b, lru_wa=lru_wa, lru_ba=lru_ba, lru_wx=lru_wx, lru_bx=lru_bx,
             lru_lambda=lru_lambda, conv_dw_w=conv_dw_w, conv_dw_b=conv_dw_b, conv_gn_g=conv_gn_g,
             conv_gn_b=conv_gn_b, conv_pw_w=conv_pw_w, conv_pw_b=conv_pw_b, w_out=w_out,
             norm2_g=norm2_g, router_w=router_w, exp_w_gate=exp_w_gate, exp_w_up=exp_w_up,
             exp_w_down=exp_w_down)
    layers = [_prepare_layer(l, p) for l in range(norm1_g.shape[0])]
    fg = final_g.reshape(1, -1)
    return (_trunk(x_prompt, layers, fg), _trunk(x_sample, layers, fg))
```

```python
import functools

import jax
import jax.numpy as jnp
from jax import lax
from jax.experimental import pallas as pl
from jax.experimental.pallas import tpu as pltpu

F32 = jnp.float32
BF16 = jnp.bfloat16

D_MODEL = 1024
MIX_POOL = 256
MIX_LRU = 512
MIX_CONV = 256
IN_COLS = MIX_POOL + 2 * MIX_LRU + 2 * MIX_CONV
POOL_GROUP_WIDTH = 64
POOL_MAX_HALF = 8
LRU_CONV_WIDTH = 4
RG_C = 8.0
CONV_WIDTH = 31
CONV_GROUPS = 4
N_EXPERTS = 16
CAPACITY_FACTOR = 2
D_FF = 11 * D_MODEL // 8
RMS_EPS = 1e-6
GN_EPS = 1e-5

SUBLANES = 8
ROW_TILE = 512
SCAN_TILE = 512
HALO = 16
TOK_BLOCK = 256
LANES = 128
BF16_ROWS = 16
COMPACT_WIN = 72
COMPACT_UNROLL = 4
FFN_ROWS = 1024
FFN_SUB = 512
COMBINE_WIN = 64
LANE_LOCAL = 64
VMEM_LIMIT = 48 * 1024 * 1024
FFN_VMEM_LIMIT = 56 * 1024 * 1024


def _cparams(*sem):
    return pltpu.CompilerParams(dimension_semantics=sem, vmem_limit_bytes=VMEM_LIMIT)


def _rms(x, g):
    return x * lax.rsqrt(jnp.mean(x * x, axis=-1, keepdims=True) + RMS_EPS) * g


def _split_bf16(x):
    hi = x.astype(BF16)
    lo = (x - hi.astype(F32)).astype(BF16)
    return hi, lo


def _dot(a, b):
    return jnp.dot(a, b, preferred_element_type=F32)


def _in_proj_kernel(x_ref, g_ref, w_ref, zp_ref, zl_ref, zg_ref, zc_ref):
    h = _rms(x_ref[...], g_ref[...])
    z = _dot(h.astype(BF16), w_ref[...])
    o1 = MIX_POOL
    o2 = o1 + MIX_LRU
    o3 = o2 + MIX_LRU
    zp_ref[...] = z[:, :o1]
    zl_ref[...] = z[:, o1:o2]
    zg_ref[...] = z[:, o2:o3]
    zc_ref[...] = z[:, o3:]


def _in_proj(x, g, w_bf16):
    n = x.shape[0]
    tm = min(ROW_TILE, n)
    row = lambda i: (i, 0)
    full = lambda i: (0, 0)
    widths = (MIX_POOL, MIX_LRU, MIX_LRU, 2 * MIX_CONV)
    return pl.pallas_call(
        _in_proj_kernel,
        grid=(n // tm,),
        in_specs=[pl.BlockSpec((tm, D_MODEL), row),
                  pl.BlockSpec((1, D_MODEL), full),
                  pl.BlockSpec((D_MODEL, IN_COLS), full)],
        out_specs=[pl.BlockSpec((tm, w), row) for w in widths],
        out_shape=[jax.ShapeDtypeStruct((n, w), F32) for w in widths],
        compiler_params=_cparams("parallel"),
        name="in_proj",
    )(x, g, w_bf16)


def _shift_rows(x, m):
    return x if m == 0 else pltpu.roll(x, x.shape[0] - m, axis=0)


def _lru_scan_kernel(*refs, tiles_per_seq, reverse):
    if reverse:
        c_ref, wa_ref, ba_ref, wx_ref, bx_ref, lam_ref, y_ref, a_ref, b_ref, h_ref = refs
    else:
        (u_ref, prev_ref, next_ref, cw_ref, cb_ref, wa_ref, ba_ref, wx_ref, bx_ref, lam_ref,
         y_ref, c_ref, a_ref, b_ref, h_ref) = refs
    t = y_ref.shape[0]
    j = pl.program_id(1)

    @pl.when(j == 0)
    def _():
        h_ref[...] = jnp.zeros_like(h_ref)

    if reverse:
        c = c_ref[...]
    else:
        ext = jnp.concatenate([jnp.where(j > 0, prev_ref[...], 0.0), u_ref[...],
                               jnp.where(j < tiles_per_seq - 1, next_ref[...], 0.0)], axis=0)
        pad_lo = LRU_CONV_WIDTH // 2
        c = cb_ref[...]
        for k in range(LRU_CONV_WIDTH):
            c = c + cw_ref[k:k + 1, :] * _shift_rows(ext, SUBLANES - pad_lo + k)[0:t, :]
        c_ref[...] = c
    cb16 = c.astype(BF16)
    r = jax.nn.sigmoid(_dot(cb16, wa_ref[...]) + ba_ref[...])
    i = jax.nn.sigmoid(_dot(cb16, wx_ref[...]) + bx_ref[...])
    neg_lam = -lam_ref[...]
    softplus = jnp.maximum(neg_lam, 0.0) + jnp.log1p(jnp.exp(-jnp.abs(neg_lam)))
    log_a = -RG_C * r * softplus
    th = jnp.tanh(log_a)
    one_minus_a2 = -2.0 * th / (1.0 - th)
    a_ref[...] = jnp.exp(log_a)
    b_ref[...] = jnp.sqrt(one_minus_a2) * (i * c)

    groups = t // SUBLANES
    sub = lax.broadcasted_iota(jnp.int32, (SUBLANES, MIX_LRU), 0)

    def body(g, carry):
        gi = groups - 1 - g if reverse else g
        start = pl.multiple_of(gi * SUBLANES, SUBLANES)
        a8 = a_ref[pl.ds(start, SUBLANES), :]
        b8 = b_ref[pl.ds(start, SUBLANES), :]
        for k in (1, 2, 4):
            shift = SUBLANES - k if reverse else k
            m = (sub < SUBLANES - k) if reverse else (sub >= k)
            a_sh = pltpu.roll(a8, shift, axis=0)
            b_sh = pltpu.roll(b8, shift, axis=0)
            b8 = jnp.where(m, a8 * b_sh + b8, b8)
            a8 = jnp.where(m, a8 * a_sh, a8)
        h = a8 * carry + b8
        y_ref[pl.ds(start, SUBLANES), :] = h
        edge = h[0:1, :] if reverse else h[SUBLANES - 1:SUBLANES, :]
        return jnp.broadcast_to(edge, (SUBLANES, MIX_LRU))

    h_ref[...] = lax.fori_loop(0, groups, body, h_ref[...], unroll=8)


def _lru_scan(src, cw, cb, wa_bd, ba, wx_bd, bx, lam, *, seq_len, reverse):
    n = src.shape[0]
    t = min(SCAN_TILE, seq_len)
    nt = seq_len // t
    nseq = n // seq_len
    halo_blocks = n // SUBLANES
    tile_blocks = t // SUBLANES

    def tile_of(s, j):
        return s * nt + ((nt - 1 - j) if reverse else j)

    def main_map(s, j):
        return (tile_of(s, j), 0)

    def prev_map(s, j):
        return (jnp.maximum(tile_of(s, j) * tile_blocks - 1, 0), 0)

    def next_map(s, j):
        return (jnp.minimum((tile_of(s, j) + 1) * tile_blocks, halo_blocks - 1), 0)

    full = lambda s, j: (0, 0)
    vec = pl.BlockSpec((1, MIX_LRU), full)
    mat = pl.BlockSpec((MIX_LRU, MIX_LRU), full)
    tile = pl.BlockSpec((t, MIX_LRU), main_map)
    out = jax.ShapeDtypeStruct((n, MIX_LRU), F32)
    gate_specs = [mat, vec, mat, vec, vec]
    if reverse:
        in_specs, args = [tile] + gate_specs, (src, wa_bd, ba, wx_bd, bx, lam)
        out_specs, out_shape = tile, out
    else:
        in_specs = [tile, pl.BlockSpec((SUBLANES, MIX_LRU), prev_map),
                    pl.BlockSpec((SUBLANES, MIX_LRU), next_map),
                    pl.BlockSpec((LRU_CONV_WIDTH, MIX_LRU), full), vec] + gate_specs
        args = (src, src, src, cw, cb, wa_bd, ba, wx_bd, bx, lam)
        out_specs, out_shape = [tile, tile], [out, out]
    return pl.pallas_call(
        functools.partial(_lru_scan_kernel, tiles_per_seq=nt, reverse=reverse),
        grid=(nseq, nt),
        in_specs=in_specs,
        out_specs=out_specs,
        out_shape=out_shape,
        scratch_shapes=[pltpu.VMEM((t, MIX_LRU), F32),
                        pltpu.VMEM((t, MIX_LRU), F32),
                        pltpu.VMEM((SUBLANES, MIX_LRU), F32)],
        compiler_params=_cparams("arbitrary", "arbitrary"),
        name="lru_scan_bwd" if reverse else "lru_scan_fwd",
    )(*args)


def _gelu_tanh(x):
    return 0.5 * x * (1.0 + jnp.tanh(0.7978845608028654 * (x + 0.044715 * (x * x * x))))


def _mix_out_kernel(x_ref, zp_ref, zp_prev_ref, zp_next_ref, zc_ref, zc_prev_ref, zc_next_ref,
                    zg_ref, yf_ref, yb_ref,
                    pool_w_ref, pool_s_ref, dw_w_ref, dw_b_ref, gn_g_ref, gn_b_ref, gavg_ref,
                    pw_w_ref, pw_b_ref, wout_ref, g2_ref, rw_hi_ref, rw_lo_ref,
                    x1_ref, xn_ref, aff_ref, payload_ref, *, tiles_per_seq, seq_len):
    t = x_ref.shape[0]
    jj = pl.program_id(0) % tiles_per_seq
    has_prev = jj > 0
    has_next = jj < tiles_per_seq - 1

    pext = jnp.concatenate([jnp.where(has_prev, zp_prev_ref[...], 0.0), zp_ref[...],
                            jnp.where(has_next, zp_next_ref[...], 0.0)], axis=0)
    lane = lax.broadcasted_iota(jnp.int32, (1, MIX_POOL), 1)
    half = jnp.left_shift(1, lane // POOL_GROUP_WIDTH)
    acc = jnp.zeros((t, MIX_POOL), F32)
    for m in range(SUBLANES):
        shifted = _shift_rows(pext, m)
        for off in range(HALO - POOL_MAX_HALF + m, HALO + POOL_MAX_HALF, SUBLANES):
            d = off - HALO
            in_window = jnp.where((d >= -half) & (d < half), 1.0, 0.0)
            acc = acc + in_window * shifted[off - m:off - m + t, :]
    pos = jj * t + lax.broadcasted_iota(jnp.int32, (t, MIX_POOL), 0)
    count = jnp.minimum(pos + half, seq_len) - jnp.maximum(pos - half, 0)
    dmean = acc / count.astype(F32) - zp_ref[...]
    y_pool = _dot(dmean.astype(BF16), pool_w_ref[...]) * pool_s_ref[...]

    def glu(z):
        return z[:, :MIX_CONV] * jax.nn.sigmoid(z[:, MIX_CONV:])

    vext = jnp.concatenate([glu(jnp.where(has_prev, zc_prev_ref[...], 0.0)), glu(zc_ref[...]),
                            glu(jnp.where(has_next, zc_next_ref[...], 0.0))], axis=0)
    v = jnp.zeros((t, MIX_CONV), F32) + dw_b_ref[...]
    first_off = HALO - CONV_WIDTH // 2
    for m in range(SUBLANES):
        shifted = _shift_rows(vext, m)
        for off in range(first_off + (m - first_off) % SUBLANES, first_off + CONV_WIDTH, SUBLANES):
            k = off - first_off
            v = v + dw_w_ref[k:k + 1, :] * shifted[off - m:off - m + t, :]
    gavg = gavg_ref[...]
    v_hi, v_lo = _split_bf16(v)
    mu = _dot(v_hi, gavg) + _dot(v_lo, gavg)
    dv = v - mu
    sq_hi, sq_lo = _split_bf16(dv * dv)
    var = _dot(sq_hi, gavg) + _dot(sq_lo, gavg)
    yn = dv * lax.rsqrt(var + GN_EPS) * gn_g_ref[...] + gn_b_ref[...]
    sw = yn * jax.nn.sigmoid(yn)
    y_conv = _dot(sw.astype(BF16), pw_w_ref[...]) + pw_b_ref[...]

    y_lru = (yf_ref[...] + yb_ref[...]) * _gelu_tanh(zg_ref[...])

    o1 = MIX_POOL
    o2 = o1 + MIX_LRU
    mixed = (_dot(y_pool.astype(BF16), wout_ref[0:o1, :])
             + _dot(y_lru.astype(BF16), wout_ref[o1:o2, :])
             + _dot(y_conv.astype(BF16), wout_ref[o2:, :]))
    x1 = x_ref[...] + mixed
    x1_ref[...] = x1

    xn = _rms(x1, g2_ref[...])
    xn_ref[...] = xn
    xn_hi, xn_lo = _split_bf16(xn)
    nt_dims = (((1,), (1,)), ((), ()))
    logits = (lax.dot_general(rw_hi_ref[...], xn_hi, nt_dims, preferred_element_type=F32)
              + lax.dot_general(rw_hi_ref[...], xn_lo, nt_dims, preferred_element_type=F32)
              + lax.dot_general(rw_lo_ref[...], xn_hi, nt_dims, preferred_element_type=F32))
    e = jnp.exp(logits - jnp.max(logits, axis=0, keepdims=True))
    aff = e / jnp.sum(e, axis=0, keepdims=True)
    aff_ref[...] = aff

    p0 = aff.astype(BF16).astype(F32)
    r1 = aff - p0
    p1 = r1.astype(BF16).astype(F32)
    p2 = r1 - p1
    tok = pl.program_id(0) * t + lax.broadcasted_iota(jnp.int32, (SUBLANES, t), 1)
    which = lax.broadcasted_iota(jnp.int32, (SUBLANES, t), 0)
    block = jnp.right_shift(tok, TOK_BLOCK.bit_length() - 1)
    meta = jnp.where(which == 0, tok & (TOK_BLOCK - 1), jnp.where(which == 1, block, 0))
    rows = [p0, p1, p2, jnp.zeros((LANE_LOCAL - 3 * N_EXPERTS, t), F32), meta.astype(F32),
            jnp.zeros((LANES - LANE_LOCAL - SUBLANES, t), F32)]
    payload_ref[...] = jnp.concatenate(rows, axis=0).T.astype(BF16)


def _mix_out(x, zp, zc, zg, yf, yb, wts, *, seq_len):
    n = x.shape[0]
    t = min(ROW_TILE, seq_len)
    nt = seq_len // t
    halo_blocks = n // HALO
    tile_blocks = t // HALO
    row = lambda i: (i, 0)
    prev = lambda i: (jnp.maximum(i * tile_blocks - 1, 0), 0)
    nxt = lambda i: (jnp.minimum((i + 1) * tile_blocks, halo_blocks - 1), 0)
    full = lambda i: (0, 0)

    def wspec(a):
        return pl.BlockSpec(a.shape, full)

    kern = functools.partial(_mix_out_kernel, tiles_per_seq=nt, seq_len=seq_len)
    return pl.pallas_call(
        kern,
        grid=(n // t,),
        in_specs=[pl.BlockSpec((t, D_MODEL), row),
                  pl.BlockSpec((t, MIX_POOL), row),
                  pl.BlockSpec((HALO, MIX_POOL), prev),
                  pl.BlockSpec((HALO, MIX_POOL), nxt),
                  pl.BlockSpec((t, 2 * MIX_CONV), row),
                  pl.BlockSpec((HALO, 2 * MIX_CONV), prev),
                  pl.BlockSpec((HALO, 2 * MIX_CONV), nxt),
                  pl.BlockSpec((t, MIX_LRU), row),
                  pl.BlockSpec((t, MIX_LRU), row),
                  pl.BlockSpec((t, MIX_LRU), row)] + [wspec(a) for a in wts],
        out_specs=[pl.BlockSpec((t, D_MODEL), row),
                   pl.BlockSpec((t, D_MODEL), row),
                   pl.BlockSpec((N_EXPERTS, t), lambda i: (0, i)),
                   pl.BlockSpec((t, LANES), row)],
        out_shape=[jax.ShapeDtypeStruct((n, D_MODEL), F32),
                   jax.ShapeDtypeStruct((n, D_MODEL), F32),
                   jax.ShapeDtypeStruct((N_EXPERTS, n), F32),
                   jax.ShapeDtypeStruct((n, LANES), BF16)],
        compiler_params=_cparams("parallel"),
        name="mix_out",
    )(x, zp, zp, zp, zc, zc, zc, zg, yf, yb, *wts)


def _select_kernel(aff_ref, gpos_ref, tot_ref, *, capacity, n_tokens):
    n_exp, nb = aff_ref.shape[0], aff_ref.shape[1]
    cap = jnp.float32(capacity)
    experts = range(n_exp)

    def count(mask):
        return jnp.sum(jnp.where(mask, 1.0, 0.0), keepdims=True)

    def as_f32(bits):
        return lax.bitcast_convert_type(bits, F32)

    def value_step(it, bits):
        bit = jnp.left_shift(1, 30 - it)
        out = []
        for e in experts:
            cand = bits[e] | bit
            out.append(jnp.where(count(aff_ref[e] >= as_f32(cand)) >= cap, cand, bits[e]))
        return tuple(out)

    zero = jnp.zeros((1, 1), jnp.int32)
    bits = lax.fori_loop(0, 31, value_step, (zero,) * n_exp)

    def masks(e):
        v = aff_ref[e]
        above = v >= as_f32(bits[e] + 1)
        return above, (v >= as_f32(bits[e])) & jnp.logical_not(above)

    need = [cap - count(masks(e)[0]) for e in experts]
    idx = (lax.broadcasted_iota(jnp.int32, (nb, TOK_BLOCK), 0) * TOK_BLOCK
           + lax.broadcasted_iota(jnp.int32, (nb, TOK_BLOCK), 1))
    index_bits = (n_tokens - 1).bit_length()

    def index_step(it, last):
        bit = jnp.left_shift(1, index_bits - 1 - it)
        out = []
        for e in experts:
            cand = last[e] | bit
            out.append(jnp.where(count(masks(e)[1] & (idx < cand)) < need[e], cand, last[e]))
        return tuple(out)

    last = lax.fori_loop(0, index_bits, index_step, (zero,) * n_exp)

    r = lax.broadcasted_iota(jnp.int32, (TOK_BLOCK, TOK_BLOCK), 0)
    c = lax.broadcasted_iota(jnp.int32, (TOK_BLOCK, TOK_BLOCK), 1)
    before = jnp.where(r < c, 1.0, 0.0).astype(BF16)
    ones = jnp.ones((TOK_BLOCK, TOK_BLOCK), BF16)
    rb = lax.broadcasted_iota(jnp.int32, (nb, nb), 0)
    cb = lax.broadcasted_iota(jnp.int32, (nb, nb), 1)
    earlier = jnp.where(cb < rb, 1.0, 0.0).astype(BF16)
    for e in experts:
        above, tie = masks(e)
        sel = above | (tie & (idx <= last[e]))
        sel16 = jnp.where(sel, 1.0, 0.0).astype(BF16)
        within = _dot(sel16, before)
        tot = _dot(sel16, ones)
        base = _dot(earlier, tot.astype(BF16))
        gpos_ref[e] = jnp.where(sel, base + within, -1.0)
        tot_ref[e] = tot[:, :LANES]


def _select(aff3, capacity):
    e, nb, _ = aff3.shape
    kern = functools.partial(_select_kernel, capacity=capacity, n_tokens=nb * TOK_BLOCK)
    whole = lambda i: (0, 0, 0)
    return pl.pallas_call(
        kern,
        grid=(1,),
        in_specs=[pl.BlockSpec((e, nb, TOK_BLOCK), whole)],
        out_specs=[pl.BlockSpec((e, nb, TOK_BLOCK), whole),
                   pl.BlockSpec((e, nb, LANES), whole)],
        out_shape=[jax.ShapeDtypeStruct((e, nb, TOK_BLOCK), F32),
                   jax.ShapeDtypeStruct((e, nb, LANES), F32)],
        compiler_params=_cparams("arbitrary"),
        name="select",
    )(aff3)


def _compact_kernel(base_ref, nwin_ref, gpos_ref, payload_ref, out_ref, tokrow_ref):
    e = pl.program_id(0)
    nb = gpos_ref.shape[0]
    out_ref[...] = jnp.zeros_like(out_ref)
    slot = lax.broadcasted_iota(jnp.int32, (COMPACT_WIN, TOK_BLOCK), 0).astype(F32)

    def moved(b, w):
        start = pl.multiple_of((base_ref[e * nb + b] // SUBLANES) * SUBLANES + w * COMPACT_WIN, SUBLANES)
        rows = payload_ref[pl.ds(pl.multiple_of(b * TOK_BLOCK, TOK_BLOCK), TOK_BLOCK), :]
        onehot = jnp.where(slot == gpos_ref[b] - start.astype(F32), 1.0, 0.0).astype(BF16)
        return start, _dot(onehot, rows)

    def group(g, carry):
        blocks = [g * COMPACT_UNROLL + u for u in range(COMPACT_UNROLL)]
        for start, rows in [moved(b, 0) for b in blocks]:
            out_ref[pl.ds(start, COMPACT_WIN), :] += rows

        for b in blocks:
            def window(w, c, b=b):
                start, rows = moved(b, w)
                out_ref[pl.ds(start, COMPACT_WIN), :] += rows
                return c
            lax.fori_loop(1, nwin_ref[e * nb + b], window, 0)
        return carry

    lax.fori_loop(0, nb // COMPACT_UNROLL, group, 0)

    capacity = tokrow_ref.shape[1]
    pick = lax.broadcasted_iota(jnp.int32, (SUBLANES, LANES), 1) - LANE_LOCAL
    pick = jnp.where(pick == lax.broadcasted_iota(jnp.int32, (SUBLANES, LANES), 0), 1.0, 0.0)
    tokrow_ref[...] = lax.dot_general(pick.astype(BF16), out_ref[0:capacity, :].astype(BF16),
                                      (((1,), (1,)), ((), ())), preferred_element_type=F32)


def _compact(base, nwin, gpos4, payload, capacity):
    n_exp, nb = gpos4.shape[0], gpos4.shape[1]
    n = payload.shape[0]
    rows = capacity + COMPACT_WIN
    grid_spec = pltpu.PrefetchScalarGridSpec(
        num_scalar_prefetch=2,
        grid=(n_exp,),
        in_specs=[pl.BlockSpec((None, nb, 1, TOK_BLOCK), lambda e, b, w: (e, 0, 0, 0)),
                  pl.BlockSpec((n, LANES), lambda e, b, w: (0, 0))],
        out_specs=[pl.BlockSpec((None, rows, LANES), lambda e, b, w: (e, 0, 0)),
                   pl.BlockSpec((None, SUBLANES, capacity), lambda e, b, w: (e, 0, 0))],
    )
    return pl.pallas_call(
        _compact_kernel,
        grid_spec=grid_spec,
        out_shape=[jax.ShapeDtypeStruct((n_exp, rows, LANES), F32),
                   jax.ShapeDtypeStruct((n_exp, SUBLANES, capacity), F32)],
        compiler_params=_cparams("arbitrary"),
        name="compact",
    )(base.reshape(-1), nwin.reshape(-1), gpos4, payload)


def _ffn_kernel(idx_hbm, xn_hbm, cmp_ref, wg_ref, wu_ref, wd_ref, ye_ref,
                idx_smem, xbuf, idx_sem, row_sem, *, steps_per_expert, tf):
    e = pl.program_id(0)
    s = e * steps_per_expert + pl.program_id(1)
    total = pl.num_programs(0) * steps_per_expert
    step_rows = 2 * tf

    def idx_copy(step):
        half = step % 2
        return pltpu.make_async_copy(
            idx_hbm.at[pl.ds(pl.multiple_of(step * step_rows, step_rows), step_rows)],
            idx_smem.at[pl.ds(pl.multiple_of(half * step_rows, step_rows), step_rows)], idx_sem.at[half])

    def start_row(first_idx, buf, r):
        tok = idx_smem[first_idx + r]
        pltpu.make_async_copy(xn_hbm.at[pl.ds(tok, 1)], xbuf.at[buf, pl.ds(r, 1)],
                              row_sem.at[buf]).start()

    def wait_rows(buf):
        pltpu.make_async_copy(xn_hbm.at[pl.ds(0, tf)], xbuf.at[buf], row_sem.at[buf]).wait()

    @pl.when(s == 0)
    def _():
        idx_copy(0).start()
        idx_copy(0).wait()

        def issue(r, c):
            start_row(0, 0, r)
            return c
        lax.fori_loop(0, tf, issue, 0, unroll=8)

    @pl.when(s + 1 < total)
    def _():
        idx_copy(s + 1).start()

    lane = lax.broadcasted_iota(jnp.int32, (step_rows, LANES), 1)
    gate_lanes = ((lane & (N_EXPERTS - 1)) == e) & (lane < 3 * N_EXPERTS)
    gate = jnp.sum(jnp.where(gate_lanes, cmp_ref[...], 0.0), axis=1, keepdims=True)
    sub = min(FFN_SUB, tf)

    def chunk(buf, out_row0, next_first_idx, next_buf):
        pieces = max(1, 3 * (tf // sub) - 2)
        issued = 0

        def issue_piece(p):
            nonlocal issued
            upto = min(tf, tf * (p + 1) // pieces)
            for r in range(issued, upto):
                start_row(next_first_idx, next_buf, r)
            issued = upto

        for hi, h0 in enumerate(range(0, tf, sub)):
            xe = xbuf[buf, h0:h0 + sub, :].astype(BF16)
            g = _dot(xe, wg_ref[...])
            issue_piece(3 * hi)
            u = _dot(xe, wu_ref[...])
            issue_piece(3 * hi + 1)
            h = (g * jax.nn.sigmoid(g)) * u
            r0 = out_row0 + h0
            ye_ref[r0:r0 + sub, :] = (_dot(h.astype(BF16), wd_ref[...]) * gate[r0:r0 + sub, :]).astype(BF16)
            issue_piece(3 * hi + 2)

    this_idx = (s % 2) * step_rows
    next_idx = ((s + 1) % 2) * step_rows
    wait_rows(0)
    chunk(0, 0, this_idx + tf, 1)

    @pl.when(s + 1 < total)
    def _():
        idx_copy(s + 1).wait()

    wait_rows(1)
    chunk(1, tf, next_idx, 0)

    @pl.when(s == total - 1)
    def _():
        wait_rows(0)


def _ffn(idx, xn, cmp, wg, wu, wd, layer, capacity):
    n_exp = wg.shape[1]
    tf = min(FFN_ROWS, capacity // 2)
    per = capacity // (2 * tf)
    kern = functools.partial(_ffn_kernel, steps_per_expert=per, tf=tf)
    weights = lambda e, i: (layer, e, 0, 0)
    return pl.pallas_call(
        kern,
        grid=(n_exp, per),
        in_specs=[pl.BlockSpec(memory_space=pl.ANY),
                  pl.BlockSpec(memory_space=pl.ANY),
                  pl.BlockSpec((None, 2 * tf, LANES), lambda e, i: (e, i, 0)),
                  pl.BlockSpec((None, None, D_MODEL, D_FF), weights),
                  pl.BlockSpec((None, None, D_MODEL, D_FF), weights),
                  pl.BlockSpec((None, None, D_FF, D_MODEL), weights)],
        out_specs=pl.BlockSpec((2 * tf, D_MODEL), lambda e, i: (e * per + i, 0)),
        out_shape=jax.ShapeDtypeStruct((n_exp * capacity, D_MODEL), BF16),
        scratch_shapes=[pltpu.SMEM((4 * tf,), jnp.int32),
                        pltpu.VMEM((2, tf, D_MODEL), F32),
                        pltpu.SemaphoreType.DMA((2,)),
                        pltpu.SemaphoreType.DMA((2,))],
        compiler_params=pltpu.CompilerParams(dimension_semantics=("arbitrary", "arbitrary"),
                                             vmem_limit_bytes=FFN_VMEM_LIMIT),
        name="ffn",
    )(idx, xn, cmp, wg, wu, wd)


def _combine_kernel(first_ref, npass_ref, grow_ref, spread_ref, ye_hbm, x1_ref, fg_ref, out_ref,
                    buf, buf_extra, acc_ref, sem, sem_extra, *, total_rows, final_norm):
    i = pl.program_id(0)
    nblocks = pl.num_programs(0)
    slot = i % 2
    last_start = total_rows - COMBINE_WIN

    def win_start(blk, e, p):
        return pl.multiple_of(jnp.minimum(first_ref[blk * N_EXPERTS + e] + p * COMBINE_WIN, last_start),
                              BF16_ROWS)

    def fetch(blk, p, dst, dsem):
        for e in range(N_EXPERTS):
            pltpu.make_async_copy(ye_hbm.at[pl.ds(win_start(blk, e, p), COMBINE_WIN)],
                                  dst.at[pl.ds(e * COMBINE_WIN, COMBINE_WIN)], dsem).start()

    def wait(dst, dsem):
        pltpu.make_async_copy(ye_hbm.at[pl.ds(0, N_EXPERTS * COMBINE_WIN)], dst, dsem).wait()

    @pl.when(i == 0)
    def _():
        fetch(0, 0, buf.at[0], sem.at[0])

    @pl.when(i + 1 < nblocks)
    def _():
        fetch(i + 1, 0, buf.at[1 - slot], sem.at[1 - slot])

    v = grow_ref[...]
    hi = jnp.floor(v * (1.0 / 256.0))
    lo = v - hi * 256.0
    spread = spread_ref[...]
    row = _dot(hi.astype(BF16), spread) * 256.0 + _dot(lo.astype(BF16), spread) - 1.0
    width = N_EXPERTS * COMBINE_WIN
    lane = lax.broadcasted_iota(jnp.int32, (1, width), 1)
    lane_blk = lane >> (COMBINE_WIN.bit_length() - 1)
    lane_in = (lane & (COMBINE_WIN - 1)).astype(F32)

    def onehot(p):
        start = jnp.zeros((1, width), F32)
        lower = jnp.zeros((1, width), F32)
        for e in range(N_EXPERTS):
            start = jnp.where(lane_blk == e, win_start(i, e, p).astype(F32), start)
            lower = jnp.where(lane_blk == e,
                              (first_ref[i * N_EXPERTS + e] + p * COMBINE_WIN).astype(F32), lower)
        rel = row - start
        if not isinstance(p, int) or p > 0:
            rel = jnp.where(row >= lower, rel, -1.0)
        return jnp.where(rel == lane_in, 1.0, 0.0).astype(BF16)

    wait(buf.at[slot], sem.at[slot])
    acc_ref[...] = _dot(onehot(0), buf[slot])

    def extra_pass(p, carry):
        fetch(i, p, buf_extra, sem_extra.at[0])
        wait(buf_extra, sem_extra.at[0])
        acc_ref[...] += _dot(onehot(p), buf_extra[...])
        return carry

    lax.fori_loop(1, npass_ref[i], extra_pass, 0)

    y = x1_ref[...] + acc_ref[...]
    if final_norm:
        y = _rms(y, fg_ref[...])
    out_ref[...] = y


def _combine(first, npass, grow, ye, x1, final_g, final_norm):
    n = x1.shape[0]
    width = N_EXPERTS * COMBINE_WIN
    lane = jnp.arange(width) // COMBINE_WIN
    spread = (jnp.arange(LANES)[:, None] == lane[None, :]).astype(BF16)
    tok = lambda i, f, p: (i, 0)
    full = lambda i, f, p: (0, 0)
    grid_spec = pltpu.PrefetchScalarGridSpec(
        num_scalar_prefetch=2,
        grid=(n // TOK_BLOCK,),
        in_specs=[pl.BlockSpec((TOK_BLOCK, LANES), tok),
                  pl.BlockSpec((LANES, width), full),
                  pl.BlockSpec(memory_space=pl.ANY),
                  pl.BlockSpec((TOK_BLOCK, D_MODEL), tok),
                  pl.BlockSpec((1, D_MODEL), full)],
        out_specs=pl.BlockSpec((TOK_BLOCK, D_MODEL), tok),
        scratch_shapes=[pltpu.VMEM((2, width, D_MODEL), BF16),
                        pltpu.VMEM((width, D_MODEL), BF16),
                        pltpu.VMEM((TOK_BLOCK, D_MODEL), F32),
                        pltpu.SemaphoreType.DMA((2,)),
                        pltpu.SemaphoreType.DMA((1,))],
    )
    kern = functools.partial(_combine_kernel, total_rows=ye.shape[0], final_norm=final_norm)
    return pl.pallas_call(
        kern,
        grid_spec=grid_spec,
        out_shape=jax.ShapeDtypeStruct((n, D_MODEL), F32),
        compiler_params=_cparams("arbitrary"),
        name="combine",
    )(first.reshape(-1), npass, grow, spread, ye, x1, final_g)


def _block_diag(w):
    g, d, _ = w.shape
    eye = jnp.eye(g, dtype=w.dtype)
    return (eye[:, None, :, None] * w[:, :, None, :]).reshape(g * d, g * d)


def _group_mean_matrix(channels, groups):
    gid = jnp.arange(channels) // (channels // groups)
    return jnp.where(gid[:, None] == gid[None, :], groups / channels, 0.0).astype(BF16)


def _prepare_layer(l, p):
    row = lambda a: a.reshape(1, -1)
    rw_t = p["router_w"][l].T
    rw_hi = rw_t.astype(BF16)
    rw_lo = (rw_t - rw_hi.astype(F32)).astype(BF16)
    lru = []
    for d in range(2):
        lru.append((p["lru_conv_w"][l], row(p["lru_conv_b"][l]),
                    _block_diag(p["lru_wa"][l, d]).astype(BF16), row(p["lru_ba"][l, d]),
                    _block_diag(p["lru_wx"][l, d]).astype(BF16), row(p["lru_bx"][l, d]),
                    row(p["lru_lambda"][l, d])))
    mix = (_block_diag(p["pool_w"][l]).astype(BF16), row(p["pool_scale"][l]),
           p["conv_dw_w"][l], row(p["conv_dw_b"][l]), row(p["conv_gn_g"][l]), row(p["conv_gn_b"][l]),
           _group_mean_matrix(MIX_CONV, CONV_GROUPS),
           p["conv_pw_w"][l].astype(BF16), row(p["conv_pw_b"][l]),
           p["w_out"][l].astype(BF16), row(p["norm2_g"][l]), rw_hi, rw_lo)
    return dict(norm1_g=row(p["norm1_g"][l]), w_in=p["w_in"][l].astype(BF16), lru=lru, mix=mix, layer=l)


def _layer(x, lw, expert_w, final_g, *, seq_len, final_norm):
    n = x.shape[0]
    zp, zl, zg, zc = _in_proj(x, lw["norm1_g"], lw["w_in"])
    yf, conv = _lru_scan(zl, *lw["lru"][0], seq_len=seq_len, reverse=False)
    yb = _lru_scan(conv, *lw["lru"][1], seq_len=seq_len, reverse=True)
    x1, xn, aff, payload = _mix_out(x, zp, zc, zg, yf, yb, lw["mix"], seq_len=seq_len)

    capacity = max(1, CAPACITY_FACTOR * n // N_EXPERTS)
    nb = n // TOK_BLOCK
    gpos, tot = _select(aff.reshape(N_EXPERTS, nb, TOK_BLOCK), capacity)

    cnt = tot[:, :, 0].astype(jnp.int32)
    base = jnp.cumsum(cnt, axis=1) - cnt
    nwin = jnp.where(cnt > 0, (base % SUBLANES + cnt + COMPACT_WIN - 1) // COMPACT_WIN, 0)
    cmp, tokrow = _compact(base, nwin, gpos.reshape(N_EXPERTS, nb, 1, TOK_BLOCK), payload, capacity)
    idx = tokrow[:, 1, :] * TOK_BLOCK + tokrow[:, 0, :]
    idx = jnp.clip(idx.astype(jnp.int32), 0, n - 1).reshape(-1)
    ye = _ffn(idx, xn, cmp, *expert_w, lw["layer"], capacity)

    expert_row0 = jnp.arange(N_EXPERTS, dtype=jnp.int32)[:, None] * capacity
    gpos2 = gpos.reshape(N_EXPERTS, n)
    grow = jnp.where(gpos2 >= 0, gpos2 + expert_row0.astype(F32) + 1.0, 0.0).T
    grow = jnp.pad(grow, ((0, 0), (0, LANES - N_EXPERTS)))
    row0 = expert_row0 + base
    first = (row0 // BF16_ROWS) * BF16_ROWS
    passes = jnp.where(cnt > 0, (row0 % BF16_ROWS + cnt + COMBINE_WIN - 1) // COMBINE_WIN, 1)
    return _combine(first.T, jnp.max(passes, axis=0), grow, ye, x1, final_g, final_norm)


def _trunk(x, layers, expert_w, final_g):
    b, s, d = x.shape
    h = x.reshape(b * s, d)
    for l, lw in enumerate(layers):
        h = _layer(h, lw, expert_w, final_g, seq_len=s, final_norm=(l == len(layers) - 1))
    return h.reshape(b, s, d)


def kernel(x_prompt, x_sample, norm1_g, w_in, pool_w, pool_scale, lru_conv_w, lru_conv_b, lru_wa, lru_ba, lru_wx, lru_bx, lru_lambda, conv_dw_w, conv_dw_b, conv_gn_g, conv_gn_b, conv_pw_w, conv_pw_b, w_out, norm2_g, router_w, exp_w_gate, exp_w_up, exp_w_down, final_g):
    p = dict(norm1_g=norm1_g, w_in=w_in, pool_w=pool_w, pool_scale=pool_scale, lru_conv_w=lru_conv_w,
             lru_conv_b=lru_conv_b, lru_wa=lru_wa, lru_ba=lru_ba, lru_wx=lru_wx, lru_bx=lru_bx,
             lru_lambda=lru_lambda, conv_dw_w=conv_dw_w, conv_dw_b=conv_dw_b, conv_gn_g=conv_gn_g,
             conv_gn_b=conv_gn_b, conv_pw_w=conv_pw_w, conv_pw_b=conv_pw_b, w_out=w_out,
             norm2_g=norm2_g, router_w=router_w, exp_w_gate=exp_w_gate, exp_w_up=exp_w_up,
             exp_w_down=exp_w_down)
    layers = [_prepare_layer(l, p) for l in range(norm1_g.shape[0])]
    fg = final_g.reshape(1, -1)
    expert_w = (exp_w_gate.astype(BF16), exp_w_up.astype(BF16), exp_w_down.astype(BF16))
    return (_trunk(x_prompt, layers, expert_w, fg), _trunk(x_sample, layers, expert_w, fg))
```

```python
import functools
import math

import jax
import jax.numpy as jnp
from jax import lax
from jax.experimental import pallas as pl
from jax.experimental.pallas import tpu as pltpu

F32 = jnp.float32
BF16 = jnp.bfloat16

D_MODEL = 1024
MIX_POOL = 256
MIX_LRU = 512
MIX_CONV = 256
IN_COLS = MIX_POOL + 2 * MIX_LRU + 2 * MIX_CONV
POOL_GROUP_WIDTH = 64
LRU_CONV_WIDTH = 4
RG_C = 8.0
CONV_WIDTH = 31
CONV_GROUPS = 4
N_EXPERTS = 16
CAPACITY_FACTOR = 2
D_FF = 11 * D_MODEL // 8
RMS_EPS = 1e-6
GN_EPS = 1e-5

SUBLANES = 8
ROW_TILE = 512
SCAN_TILE = 512
HALO = 16
TOK_BLOCK = 256
LANES = 128
BF16_ROWS = 16
COMPACT_WIN = 72
COMPACT_UNROLL = 8
FFN_ROWS = 1024
FFN_SUB = 512
COMBINE_WIN = 64
LANE_LOCAL = 64
VMEM_LIMIT = 48 * 1024 * 1024
FFN_VMEM_LIMIT = 56 * 1024 * 1024


def _cparams(*sem):
    return pltpu.CompilerParams(dimension_semantics=sem, vmem_limit_bytes=VMEM_LIMIT)


def _rms(x, g):
    return x * lax.rsqrt(jnp.mean(x * x, axis=-1, keepdims=True) + RMS_EPS) * g


def _split_bf16(x):
    hi = x.astype(BF16)
    lo = (x - hi.astype(F32)).astype(BF16)
    return hi, lo


def _dot(a, b):
    return jnp.dot(a, b, preferred_element_type=F32)


def _in_proj_kernel(x_ref, g_ref, w_ref, zp_ref, zl_ref, zg_ref, zc_ref):
    h = _rms(x_ref[...], g_ref[...])
    z = _dot(h.astype(BF16), w_ref[...])
    o1 = MIX_POOL
    o2 = o1 + MIX_LRU
    o3 = o2 + MIX_LRU
    zp_ref[...] = z[:, :o1]
    zl_ref[...] = z[:, o1:o2]
    zg_ref[...] = z[:, o2:o3]
    zc_ref[...] = z[:, o3:]


def _in_proj(x, g, w_bf16):
    n = x.shape[0]
    tm = min(ROW_TILE, n)
    row = lambda i: (i, 0)
    full = lambda i: (0, 0)
    widths = (MIX_POOL, MIX_LRU, MIX_LRU, 2 * MIX_CONV)
    return pl.pallas_call(
        _in_proj_kernel,
        grid=(n // tm,),
        in_specs=[pl.BlockSpec((tm, D_MODEL), row),
                  pl.BlockSpec((1, D_MODEL), full),
                  pl.BlockSpec((D_MODEL, IN_COLS), full)],
        out_specs=[pl.BlockSpec((tm, w), row) for w in widths],
        out_shape=[jax.ShapeDtypeStruct((n, w), F32) for w in widths],
        compiler_params=_cparams("parallel"),
        name="in_proj",
    )(x, g, w_bf16)


def _shift_rows(x, m):
    return x if m == 0 else pltpu.roll(x, (-m) % x.shape[0], axis=0)


def _lru_scan_kernel(*refs, tiles_per_seq, reverse):
    if reverse:
        c_ref, wa_ref, ba_ref, wx_ref, bx_ref, lam_ref, y_ref, a_ref, b_ref, h_ref = refs
    else:
        (u_ref, prev_ref, next_ref, cw_ref, cb_ref, wa_ref, ba_ref, wx_ref, bx_ref, lam_ref,
         y_ref, c_ref, a_ref, b_ref, h_ref) = refs
    t = y_ref.shape[0]
    j = pl.program_id(1)

    @pl.when(j == 0)
    def _():
        h_ref[...] = jnp.zeros_like(h_ref)

    if reverse:
        c = c_ref[...]
    else:
        ext = jnp.concatenate([jnp.where(j > 0, prev_ref[...], 0.0), u_ref[...],
                               jnp.where(j < tiles_per_seq - 1, next_ref[...], 0.0)], axis=0)
        pad_lo = LRU_CONV_WIDTH // 2
        c = cb_ref[...]
        for k in range(LRU_CONV_WIDTH):
            c = c + cw_ref[k:k + 1, :] * _shift_rows(ext, SUBLANES - pad_lo + k)[0:t, :]
        c_ref[...] = c
    cb16 = c.astype(BF16)
    r = jax.nn.sigmoid(_dot(cb16, wa_ref[...]) + ba_ref[...])
    i = jax.nn.sigmoid(_dot(cb16, wx_ref[...]) + bx_ref[...])
    neg_lam = -lam_ref[...]
    softplus = jnp.maximum(neg_lam, 0.0) + jnp.log1p(jnp.exp(-jnp.abs(neg_lam)))
    log_a = -RG_C * r * softplus
    th = jnp.tanh(log_a)
    one_minus_a2 = -2.0 * th / (1.0 - th)
    a_ref[...] = jnp.exp(log_a)
    b_ref[...] = jnp.sqrt(one_minus_a2) * (i * c)

    groups = t // SUBLANES
    sub = lax.broadcasted_iota(jnp.int32, (SUBLANES, MIX_LRU), 0)

    def body(g, carry):
        gi = groups - 1 - g if reverse else g
        start = pl.multiple_of(gi * SUBLANES, SUBLANES)
        a8 = a_ref[pl.ds(start, SUBLANES), :]
        b8 = b_ref[pl.ds(start, SUBLANES), :]
        for k in (1, 2, 4):
            shift = SUBLANES - k if reverse else k
            m = (sub < SUBLANES - k) if reverse else (sub >= k)
            a_sh = pltpu.roll(a8, shift, axis=0)
            b_sh = pltpu.roll(b8, shift, axis=0)
            b8 = jnp.where(m, a8 * b_sh + b8, b8)
            a8 = jnp.where(m, a8 * a_sh, a8)
        h = a8 * carry + b8
        y_ref[pl.ds(start, SUBLANES), :] = h
        edge = h[0:1, :] if reverse else h[SUBLANES - 1:SUBLANES, :]
        return jnp.broadcast_to(edge, (SUBLANES, MIX_LRU))

    h_ref[...] = lax.fori_loop(0, groups, body, h_ref[...], unroll=8)


def _lru_scan(src, cw, cb, wa_bd, ba, wx_bd, bx, lam, *, seq_len, reverse):
    n = src.shape[0]
    t = min(SCAN_TILE, seq_len)
    nt = seq_len // t
    nseq = n // seq_len
    halo_blocks = n // SUBLANES
    tile_blocks = t // SUBLANES

    def tile_of(s, j):
        return s * nt + ((nt - 1 - j) if reverse else j)

    def main_map(s, j):
        return (tile_of(s, j), 0)

    def prev_map(s, j):
        return (jnp.maximum(tile_of(s, j) * tile_blocks - 1, 0), 0)

    def next_map(s, j):
        return (jnp.minimum((tile_of(s, j) + 1) * tile_blocks, halo_blocks - 1), 0)

    full = lambda s, j: (0, 0)
    vec = pl.BlockSpec((1, MIX_LRU), full)
    mat = pl.BlockSpec((MIX_LRU, MIX_LRU), full)
    tile = pl.BlockSpec((t, MIX_LRU), main_map)
    out = jax.ShapeDtypeStruct((n, MIX_LRU), F32)
    gate_specs = [mat, vec, mat, vec, vec]
    if reverse:
        in_specs, args = [tile] + gate_specs, (src, wa_bd, ba, wx_bd, bx, lam)
        out_specs, out_shape = tile, out
    else:
        in_specs = [tile, pl.BlockSpec((SUBLANES, MIX_LRU), prev_map),
                    pl.BlockSpec((SUBLANES, MIX_LRU), next_map),
                    pl.BlockSpec((LRU_CONV_WIDTH, MIX_LRU), full), vec] + gate_specs
        args = (src, src, src, cw, cb, wa_bd, ba, wx_bd, bx, lam)
        out_specs, out_shape = [tile, tile], [out, out]
    return pl.pallas_call(
        functools.partial(_lru_scan_kernel, tiles_per_seq=nt, reverse=reverse),
        grid=(nseq, nt),
        in_specs=in_specs,
        out_specs=out_specs,
        out_shape=out_shape,
        scratch_shapes=[pltpu.VMEM((t, MIX_LRU), F32),
                        pltpu.VMEM((t, MIX_LRU), F32),
                        pltpu.VMEM((SUBLANES, MIX_LRU), F32)],
        compiler_params=_cparams("arbitrary", "arbitrary"),
        name="lru_scan_bwd" if reverse else "lru_scan_fwd",
    )(*args)


def _gelu_tanh(x):
    return 0.5 * x * (1.0 + jnp.tanh(0.7978845608028654 * (x + 0.044715 * (x * x * x))))


def _mix_out_kernel(x_ref, zp_ref, zp_prev_ref, zp_next_ref, zc_ref, zc_prev_ref, zc_next_ref,
                    zg_ref, yf_ref, yb_ref,
                    pool_w_ref, pool_s_ref, dw_w_ref, dw_b_ref, gn_g_ref, gn_b_ref, gavg_ref,
                    pw_w_ref, pw_b_ref, wout_ref, g2_ref, rw_hi_ref, rw_lo_ref,
                    x1_ref, xn_ref, aff_ref, payload_ref, *, tiles_per_seq, seq_len):
    t = x_ref.shape[0]
    jj = pl.program_id(0) % tiles_per_seq
    has_prev = jj > 0
    has_next = jj < tiles_per_seq - 1

    pext = jnp.concatenate([jnp.where(has_prev, zp_prev_ref[...], 0.0), zp_ref[...],
                            jnp.where(has_next, zp_next_ref[...], 0.0)], axis=0)
    lane = lax.broadcasted_iota(jnp.int32, (1, MIX_POOL), 1)
    half = jnp.left_shift(1, lane // POOL_GROUP_WIDTH)
    trailing = pext
    acc = None
    for g in range(MIX_POOL // POOL_GROUP_WIDTH):
        trailing = trailing + _shift_rows(trailing, -(1 << g))
        centred = _shift_rows(trailing, (1 << g) - 1)[HALO:HALO + t, :]
        acc = centred if acc is None else jnp.where(lane // POOL_GROUP_WIDTH >= g, centred, acc)
    pos = jj * t + lax.broadcasted_iota(jnp.int32, (t, MIX_POOL), 0)
    count = jnp.minimum(pos + half, seq_len) - jnp.maximum(pos - half, 0)
    dmean = acc / count.astype(F32) - zp_ref[...]
    y_pool = _dot(dmean.astype(BF16), pool_w_ref[...]) * pool_s_ref[...]

    def glu(z):
        return z[:, :MIX_CONV] * jax.nn.sigmoid(z[:, MIX_CONV:])

    vext = jnp.concatenate([glu(jnp.where(has_prev, zc_prev_ref[...], 0.0)), glu(zc_ref[...]),
                            glu(jnp.where(has_next, zc_next_ref[...], 0.0))], axis=0)
    v = jnp.zeros((t, MIX_CONV), F32) + dw_b_ref[...]
    first_off = HALO - CONV_WIDTH // 2
    for m in range(SUBLANES):
        shifted = _shift_rows(vext, m)
        for off in range(first_off + (m - first_off) % SUBLANES, first_off + CONV_WIDTH, SUBLANES):
            k = off - first_off
            v = v + dw_w_ref[k:k + 1, :] * shifted[off - m:off - m + t, :]
    gavg = gavg_ref[...]
    v_hi, v_lo = _split_bf16(v)
    mu = _dot(v_hi, gavg) + _dot(v_lo, gavg)
    dv = v - mu
    sq_hi, sq_lo = _split_bf16(dv * dv)
    var = _dot(sq_hi, gavg) + _dot(sq_lo, gavg)
    yn = dv * lax.rsqrt(var + GN_EPS) * gn_g_ref[...] + gn_b_ref[...]
    sw = yn * jax.nn.sigmoid(yn)
    y_conv = _dot(sw.astype(BF16), pw_w_ref[...]) + pw_b_ref[...]

    y_lru = (yf_ref[...] + yb_ref[...]) * _gelu_tanh(zg_ref[...])

    o1 = MIX_POOL
    o2 = o1 + MIX_LRU
    mixed = (_dot(y_pool.astype(BF16), wout_ref[0:o1, :])
             + _dot(y_lru.astype(BF16), wout_ref[o1:o2, :])
             + _dot(y_conv.astype(BF16), wout_ref[o2:, :]))
    x1 = x_ref[...] + mixed
    x1_ref[...] = x1

    xn = _rms(x1, g2_ref[...])
    xn_ref[...] = xn
    xn_hi, xn_lo = _split_bf16(xn)
    nt_dims = (((1,), (1,)), ((), ()))
    logits = (lax.dot_general(rw_hi_ref[...], xn_hi, nt_dims, preferred_element_type=F32)
              + lax.dot_general(rw_hi_ref[...], xn_lo, nt_dims, preferred_element_type=F32)
              + lax.dot_general(rw_lo_ref[...], xn_hi, nt_dims, preferred_element_type=F32))
    e = jnp.exp(logits - jnp.max(logits, axis=0, keepdims=True))
    aff = e / jnp.sum(e, axis=0, keepdims=True)
    aff_ref[...] = aff

    p0 = aff.astype(BF16).astype(F32)
    r1 = aff - p0
    p1 = r1.astype(BF16).astype(F32)
    p2 = r1 - p1
    tok = pl.program_id(0) * t + lax.broadcasted_iota(jnp.int32, (SUBLANES, t), 1)
    which = lax.broadcasted_iota(jnp.int32, (SUBLANES, t), 0)
    block = jnp.right_shift(tok, TOK_BLOCK.bit_length() - 1)
    meta = jnp.where(which == 0, tok & (TOK_BLOCK - 1), jnp.where(which == 1, block, 0))
    rows = [p0, p1, p2, jnp.zeros((LANE_LOCAL - 3 * N_EXPERTS, t), F32), meta.astype(F32),
            jnp.zeros((LANES - LANE_LOCAL - SUBLANES, t), F32)]
    payload_ref[...] = jnp.concatenate(rows, axis=0).T.astype(BF16)


def _mix_out(x, zp, zc, zg, yf, yb, wts, *, seq_len):
    n = x.shape[0]
    t = min(ROW_TILE, seq_len)
    nt = seq_len // t
    halo_blocks = n // HALO
    tile_blocks = t // HALO
    row = lambda i: (i, 0)
    prev = lambda i: (jnp.maximum(i * tile_blocks - 1, 0), 0)
    nxt = lambda i: (jnp.minimum((i + 1) * tile_blocks, halo_blocks - 1), 0)
    full = lambda i: (0, 0)

    def wspec(a):
        return pl.BlockSpec(a.shape, full)

    kern = functools.partial(_mix_out_kernel, tiles_per_seq=nt, seq_len=seq_len)
    return pl.pallas_call(
        kern,
        grid=(n // t,),
        in_specs=[pl.BlockSpec((t, D_MODEL), row),
                  pl.BlockSpec((t, MIX_POOL), row),
                  pl.BlockSpec((HALO, MIX_POOL), prev),
                  pl.BlockSpec((HALO, MIX_POOL), nxt),
                  pl.BlockSpec((t, 2 * MIX_CONV), row),
                  pl.BlockSpec((HALO, 2 * MIX_CONV), prev),
                  pl.BlockSpec((HALO, 2 * MIX_CONV), nxt),
                  pl.BlockSpec((t, MIX_LRU), row),
                  pl.BlockSpec((t, MIX_LRU), row),
                  pl.BlockSpec((t, MIX_LRU), row)] + [wspec(a) for a in wts],
        out_specs=[pl.BlockSpec((t, D_MODEL), row),
                   pl.BlockSpec((t, D_MODEL), row),
                   pl.BlockSpec((N_EXPERTS, t), lambda i: (0, i)),
                   pl.BlockSpec((t, LANES), row)],
        out_shape=[jax.ShapeDtypeStruct((n, D_MODEL), F32),
                   jax.ShapeDtypeStruct((n, D_MODEL), F32),
                   jax.ShapeDtypeStruct((N_EXPERTS, n), F32),
                   jax.ShapeDtypeStruct((n, LANES), BF16)],
        compiler_params=_cparams("parallel"),
        name="mix_out",
    )(x, zp, zp, zp, zc, zc, zc, zg, yf, yb, *wts)


def _select_kernel(aff_ref, gpos_ref, tot_ref, *, capacity, n_tokens):
    n_exp, nb = aff_ref.shape[0], aff_ref.shape[1]
    cap = jnp.float32(capacity)
    experts = range(n_exp)

    def count(mask):
        return jnp.sum(jnp.where(mask, 1.0, 0.0), keepdims=True)

    def as_f32(bits):
        return lax.bitcast_convert_type(bits, F32)

    def value_step(it, bits):
        bit = jnp.left_shift(1, 30 - it)
        out = []
        for e in experts:
            cand = bits[e] | bit
            out.append(jnp.where(count(aff_ref[e] >= as_f32(cand)) >= cap, cand, bits[e]))
        return tuple(out)

    zero = jnp.zeros((1, 1), jnp.int32)
    bits = lax.fori_loop(0, 31, value_step, (zero,) * n_exp)

    def masks(e):
        v = aff_ref[e]
        above = v >= as_f32(bits[e] + 1)
        return above, (v >= as_f32(bits[e])) & jnp.logical_not(above)

    need = [cap - count(masks(e)[0]) for e in experts]
    idx = (lax.broadcasted_iota(jnp.int32, (nb, TOK_BLOCK), 0) * TOK_BLOCK
           + lax.broadcasted_iota(jnp.int32, (nb, TOK_BLOCK), 1))
    index_bits = (n_tokens - 1).bit_length()

    def index_step(it, last):
        bit = jnp.left_shift(1, index_bits - 1 - it)
        out = []
        for e in experts:
            cand = last[e] | bit
            out.append(jnp.where(count(masks(e)[1] & (idx < cand)) < need[e], cand, last[e]))
        return tuple(out)

    last = lax.fori_loop(0, index_bits, index_step, (zero,) * n_exp)

    r = lax.broadcasted_iota(jnp.int32, (TOK_BLOCK, TOK_BLOCK), 0)
    c = lax.broadcasted_iota(jnp.int32, (TOK_BLOCK, TOK_BLOCK), 1)
    before = jnp.where(r < c, 1.0, 0.0).astype(BF16)
    ones = jnp.ones((TOK_BLOCK, TOK_BLOCK), BF16)
    rb = lax.broadcasted_iota(jnp.int32, (nb, nb), 0)
    cb = lax.broadcasted_iota(jnp.int32, (nb, nb), 1)
    earlier = jnp.where(cb < rb, 1.0, 0.0).astype(BF16)
    for e in experts:
        above, tie = masks(e)
        sel = above | (tie & (idx <= last[e]))
        sel16 = jnp.where(sel, 1.0, 0.0).astype(BF16)
        within = _dot(sel16, before)
        tot = _dot(sel16, ones)
        base = _dot(earlier, tot.astype(BF16))
        gpos_ref[e] = jnp.where(sel, base + within, -1.0)
        tot_ref[e] = tot[:, :LANES]


def _select(aff3, capacity):
    e, nb, _ = aff3.shape
    kern = functools.partial(_select_kernel, capacity=capacity, n_tokens=nb * TOK_BLOCK)
    whole = lambda i: (0, 0, 0)
    return pl.pallas_call(
        kern,
        grid=(1,),
        in_specs=[pl.BlockSpec((e, nb, TOK_BLOCK), whole)],
        out_specs=[pl.BlockSpec((e, nb, TOK_BLOCK), whole),
                   pl.BlockSpec((e, nb, LANES), whole)],
        out_shape=[jax.ShapeDtypeStruct((e, nb, TOK_BLOCK), F32),
                   jax.ShapeDtypeStruct((e, nb, LANES), F32)],
        compiler_params=_cparams("arbitrary"),
        name="select",
    )(aff3)


def _compact_kernel(base_ref, nwin_ref, gpos_ref, payload_ref, out_ref, tokrow_ref):
    e = pl.program_id(0)
    nb = gpos_ref.shape[0]
    out_ref[...] = jnp.zeros_like(out_ref)
    slot = lax.broadcasted_iota(jnp.int32, (COMPACT_WIN, TOK_BLOCK), 0).astype(F32)

    def moved(b, w):
        start = pl.multiple_of((base_ref[e * nb + b] // SUBLANES) * SUBLANES + w * COMPACT_WIN, SUBLANES)
        rows = payload_ref[pl.ds(pl.multiple_of(b * TOK_BLOCK, TOK_BLOCK), TOK_BLOCK), :]
        onehot = jnp.where(slot == gpos_ref[b] - start.astype(F32), 1.0, 0.0).astype(BF16)
        return start, _dot(onehot, rows)

    unroll = math.gcd(COMPACT_UNROLL, nb)

    def group(g, carry):
        blocks = [g * unroll + u for u in range(unroll)]
        for start, rows in [moved(b, 0) for b in blocks]:
            out_ref[pl.ds(start, COMPACT_WIN), :] += rows

        for b in blocks:
            def window(w, c, b=b):
                start, rows = moved(b, w)
                out_ref[pl.ds(start, COMPACT_WIN), :] += rows
                return c
            lax.fori_loop(1, nwin_ref[e * nb + b], window, 0)
        return carry

    lax.fori_loop(0, nb // unroll, group, 0)

    capacity = tokrow_ref.shape[1]
    pick = lax.broadcasted_iota(jnp.int32, (SUBLANES, LANES), 1) - LANE_LOCAL
    pick = jnp.where(pick == lax.broadcasted_iota(jnp.int32, (SUBLANES, LANES), 0), 1.0, 0.0)
    tokrow_ref[...] = lax.dot_general(pick.astype(BF16), out_ref[0:capacity, :].astype(BF16),
                                      (((1,), (1,)), ((), ())), preferred_element_type=F32)


def _compact(base, nwin, gpos4, payload, capacity):
    n_exp, nb = gpos4.shape[0], gpos4.shape[1]
    n = payload.shape[0]
    rows = capacity + COMPACT_WIN
    grid_spec = pltpu.PrefetchScalarGridSpec(
        num_scalar_prefetch=2,
        grid=(n_exp,),
        in_specs=[pl.BlockSpec((None, nb, 1, TOK_BLOCK), lambda e, b, w: (e, 0, 0, 0)),
                  pl.BlockSpec((n, LANES), lambda e, b, w: (0, 0))],
        out_specs=[pl.BlockSpec((None, rows, LANES), lambda e, b, w: (e, 0, 0)),
                   pl.BlockSpec((None, SUBLANES, capacity), lambda e, b, w: (e, 0, 0))],
    )
    return pl.pallas_call(
        _compact_kernel,
        grid_spec=grid_spec,
        out_shape=[jax.ShapeDtypeStruct((n_exp, rows, LANES), F32),
                   jax.ShapeDtypeStruct((n_exp, SUBLANES, capacity), F32)],
        compiler_params=_cparams("arbitrary"),
        name="compact",
    )(base.reshape(-1), nwin.reshape(-1), gpos4, payload)


def _ffn_kernel(idx_hbm, xn_hbm, cmp_ref, wg_ref, wu_ref, wd_ref, ye_ref,
                idx_smem, xbuf, idx_sem, row_sem, *, steps_per_expert, tf):
    e = pl.program_id(0)
    s = e * steps_per_expert + pl.program_id(1)
    total = pl.num_programs(0) * steps_per_expert
    step_rows = 2 * tf

    def idx_copy(step):
        half = step % 2
        return pltpu.make_async_copy(
            idx_hbm.at[pl.ds(pl.multiple_of(step * step_rows, step_rows), step_rows)],
            idx_smem.at[pl.ds(pl.multiple_of(half * step_rows, step_rows), step_rows)], idx_sem.at[half])

    def start_row(first_idx, buf, r):
        tok = idx_smem[first_idx + r]
        pltpu.make_async_copy(xn_hbm.at[pl.ds(tok, 1)], xbuf.at[buf, pl.ds(r, 1)],
                              row_sem.at[buf]).start()

    def wait_rows(buf):
        pltpu.make_async_copy(xn_hbm.at[pl.ds(0, tf)], xbuf.at[buf], row_sem.at[buf]).wait()

    @pl.when(s == 0)
    def _():
        idx_copy(0).start()
        idx_copy(0).wait()

        def issue(r, c):
            start_row(0, 0, r)
            return c
        lax.fori_loop(0, tf, issue, 0, unroll=8)

    @pl.when(s + 1 < total)
    def _():
        idx_copy(s + 1).start()

    lane = lax.broadcasted_iota(jnp.int32, (step_rows, LANES), 1)
    gate_lanes = ((lane & (N_EXPERTS - 1)) == e) & (lane < 3 * N_EXPERTS)
    gate = jnp.sum(jnp.where(gate_lanes, cmp_ref[...], 0.0), axis=1, keepdims=True)
    sub = min(FFN_SUB, tf)

    def chunk(buf, out_row0, next_first_idx, next_buf):
        pieces = max(1, 3 * (tf // sub) - 2)
        issued = 0

        def issue_piece(p):
            nonlocal issued
            upto = min(tf, tf * (p + 1) // pieces)
            for r in range(issued, upto):
                start_row(next_first_idx, next_buf, r)
            issued = upto

        for hi, h0 in enumerate(range(0, tf, sub)):
            xe = xbuf[buf, h0:h0 + sub, :].astype(BF16)
            g = _dot(xe, wg_ref[...])
            issue_piece(3 * hi)
            u = _dot(xe, wu_ref[...])
            issue_piece(3 * hi + 1)
            h = (g * jax.nn.sigmoid(g)) * u
            r0 = out_row0 + h0
            ye_ref[r0:r0 + sub, :] = (_dot(h.astype(BF16), wd_ref[...]) * gate[r0:r0 + sub, :]).astype(BF16)
            issue_piece(3 * hi + 2)

    this_idx = (s % 2) * step_rows
    next_idx = ((s + 1) % 2) * step_rows
    wait_rows(0)
    chunk(0, 0, this_idx + tf, 1)

    @pl.when(s + 1 < total)
    def _():
        idx_copy(s + 1).wait()

    wait_rows(1)
    chunk(1, tf, next_idx, 0)

    @pl.when(s == total - 1)
    def _():
        wait_rows(0)


def _ffn(idx, xn, cmp, wg, wu, wd, layer, capacity):
    n_exp = wg.shape[1]
    tf = min(FFN_ROWS, capacity // 2)
    per = capacity // (2 * tf)
    kern = functools.partial(_ffn_kernel, steps_per_expert=per, tf=tf)
    weights = lambda e, i: (layer, e, 0, 0)
    return pl.pallas_call(
        kern,
        grid=(n_exp, per),
        in_specs=[pl.BlockSpec(memory_space=pl.ANY),
                  pl.BlockSpec(memory_space=pl.ANY),
                  pl.BlockSpec((None, 2 * tf, LANES), lambda e, i: (e, i, 0)),
                  pl.BlockSpec((None, None, D_MODEL, D_FF), weights),
                  pl.BlockSpec((None, None, D_MODEL, D_FF), weights),
                  pl.BlockSpec((None, None, D_FF, D_MODEL), weights)],
        out_specs=pl.BlockSpec((2 * tf, D_MODEL), lambda e, i: (e * per + i, 0)),
        out_shape=jax.ShapeDtypeStruct((n_exp * capacity, D_MODEL), BF16),
        scratch_shapes=[pltpu.SMEM((4 * tf,), jnp.int32),
                        pltpu.VMEM((2, tf, D_MODEL), F32),
                        pltpu.SemaphoreType.DMA((2,)),
                        pltpu.SemaphoreType.DMA((2,))],
        compiler_params=pltpu.CompilerParams(dimension_semantics=("arbitrary", "arbitrary"),
                                             vmem_limit_bytes=FFN_VMEM_LIMIT),
        name="ffn",
    )(idx, xn, cmp, wg, wu, wd)


def _combine_kernel(first_ref, npass_ref, grow_ref, spread_ref, ye_hbm, x1_ref, fg_ref, out_ref,
                    buf, buf_extra, sem, sem_extra, *, total_rows, final_norm):
    i = pl.program_id(0)
    nblocks = pl.num_programs(0)
    slot = i % 2
    last_start = total_rows - COMBINE_WIN

    def win_start(blk, e, p):
        return pl.multiple_of(jnp.minimum(first_ref[blk * N_EXPERTS + e] + p * COMBINE_WIN, last_start),
                              BF16_ROWS)

    def fetch(blk, p, dst, dsem):
        for e in range(N_EXPERTS):
            pltpu.make_async_copy(ye_hbm.at[pl.ds(win_start(blk, e, p), COMBINE_WIN)],
                                  dst.at[pl.ds(e * COMBINE_WIN, COMBINE_WIN)], dsem).start()

    def wait(dst, dsem):
        pltpu.make_async_copy(ye_hbm.at[pl.ds(0, N_EXPERTS * COMBINE_WIN)], dst, dsem).wait()

    @pl.when(i == 0)
    def _():
        fetch(0, 0, buf.at[0], sem.at[0])

    @pl.when(i + 1 < nblocks)
    def _():
        fetch(i + 1, 0, buf.at[1 - slot], sem.at[1 - slot])

    v = grow_ref[...]
    hi = jnp.floor(v * (1.0 / 256.0))
    lo = v - hi * 256.0
    spread = spread_ref[...]
    row = _dot(hi.astype(BF16), spread) * 256.0 + _dot(lo.astype(BF16), spread) - 1.0
    width = N_EXPERTS * COMBINE_WIN
    lane = lax.broadcasted_iota(jnp.int32, (1, width), 1)
    lane_blk = lane >> (COMBINE_WIN.bit_length() - 1)
    lane_in = (lane & (COMBINE_WIN - 1)).astype(F32)

    def onehot(p):
        start = jnp.zeros((1, width), F32)
        lower = jnp.zeros((1, width), F32)
        for e in range(N_EXPERTS):
            start = jnp.where(lane_blk == e, win_start(i, e, p).astype(F32), start)
            lower = jnp.where(lane_blk == e,
                              (first_ref[i * N_EXPERTS + e] + p * COMBINE_WIN).astype(F32), lower)
        rel = row - start
        if not isinstance(p, int) or p > 0:
            rel = jnp.where(row >= lower, rel, -1.0)
        return jnp.where(rel == lane_in, 1.0, 0.0).astype(BF16)

    wait(buf.at[slot], sem.at[slot])
    out_ref[...] = x1_ref[...] + _dot(onehot(0), buf[slot])

    def extra_pass(p, carry):
        fetch(i, p, buf_extra, sem_extra.at[0])
        wait(buf_extra, sem_extra.at[0])
        out_ref[...] += _dot(onehot(p), buf_extra[...])
        return carry

    lax.fori_loop(1, npass_ref[i], extra_pass, 0)

    if final_norm:
        out_ref[...] = _rms(out_ref[...], fg_ref[...])


def _combine(first, npass, grow, ye, x1, final_g, final_norm):
    n = x1.shape[0]
    width = N_EXPERTS * COMBINE_WIN
    lane = jnp.arange(width) // COMBINE_WIN
    spread = (jnp.arange(LANES)[:, None] == lane[None, :]).astype(BF16)
    tok = lambda i, f, p: (i, 0)
    full = lambda i, f, p: (0, 0)
    grid_spec = pltpu.PrefetchScalarGridSpec(
        num_scalar_prefetch=2,
        grid=(n // TOK_BLOCK,),
        in_specs=[pl.BlockSpec((TOK_BLOCK, LANES), tok),
                  pl.BlockSpec((LANES, width), full),
                  pl.BlockSpec(memory_space=pl.ANY),
                  pl.BlockSpec((TOK_BLOCK, D_MODEL), tok),
                  pl.BlockSpec((1, D_MODEL), full)],
        out_specs=pl.BlockSpec((TOK_BLOCK, D_MODEL), tok),
        scratch_shapes=[pltpu.VMEM((2, width, D_MODEL), BF16),
                        pltpu.VMEM((width, D_MODEL), BF16),
                        pltpu.SemaphoreType.DMA((2,)),
                        pltpu.SemaphoreType.DMA((1,))],
    )
    kern = functools.partial(_combine_kernel, total_rows=ye.shape[0], final_norm=final_norm)
    return pl.pallas_call(
        kern,
        grid_spec=grid_spec,
        out_shape=jax.ShapeDtypeStruct((n, D_MODEL), F32),
        compiler_params=_cparams("arbitrary"),
        name="combine",
    )(first.reshape(-1), npass, grow, spread, ye, x1, final_g)


def _block_diag(w):
    g, d, _ = w.shape
    eye = jnp.eye(g, dtype=w.dtype)
    return (eye[:, None, :, None] * w[:, :, None, :]).reshape(g * d, g * d)


def _group_mean_matrix(channels, groups):
    gid = jnp.arange(channels) // (channels // groups)
    return jnp.where(gid[:, None] == gid[None, :], groups / channels, 0.0).astype(BF16)


def _prepare_layer(l, p):
    row = lambda a: a.reshape(1, -1)
    rw_t = p["router_w"][l].T
    rw_hi = rw_t.astype(BF16)
    rw_lo = (rw_t - rw_hi.astype(F32)).astype(BF16)
    lru = []
    for d in range(2):
        lru.append((p["lru_conv_w"][l], row(p["lru_conv_b"][l]),
                    _block_diag(p["lru_wa"][l, d]).astype(BF16), row(p["lru_ba"][l, d]),
                    _block_diag(p["lru_wx"][l, d]).astype(BF16), row(p["lru_bx"][l, d]),
                    row(p["lru_lambda"][l, d])))
    mix = (_block_diag(p["pool_w"][l]).astype(BF16), row(p["pool_scale"][l]),
           p["conv_dw_w"][l], row(p["conv_dw_b"][l]), row(p["conv_gn_g"][l]), row(p["conv_gn_b"][l]),
           _group_mean_matrix(MIX_CONV, CONV_GROUPS),
           p["conv_pw_w"][l].astype(BF16), row(p["conv_pw_b"][l]),
           p["w_out"][l].astype(BF16), row(p["norm2_g"][l]), rw_hi, rw_lo)
    return dict(norm1_g=row(p["norm1_g"][l]), w_in=p["w_in"][l].astype(BF16), lru=lru, mix=mix, layer=l)


def _layer(x, lw, expert_w, final_g, *, seq_len, final_norm):
    n = x.shape[0]
    zp, zl, zg, zc = _in_proj(x, lw["norm1_g"], lw["w_in"])
    yf, conv = _lru_scan(zl, *lw["lru"][0], seq_len=seq_len, reverse=False)
    yb = _lru_scan(conv, *lw["lru"][1], seq_len=seq_len, reverse=True)
    x1, xn, aff, payload = _mix_out(x, zp, zc, zg, yf, yb, lw["mix"], seq_len=seq_len)

    capacity = max(1, CAPACITY_FACTOR * n // N_EXPERTS)
    nb = n // TOK_BLOCK
    gpos, tot = _select(aff.reshape(N_EXPERTS, nb, TOK_BLOCK), capacity)

    cnt = tot[:, :, 0].astype(jnp.int32)
    base = jnp.cumsum(cnt, axis=1) - cnt
    nwin = jnp.where(cnt > 0, (base % SUBLANES + cnt + COMPACT_WIN - 1) // COMPACT_WIN, 0)
    cmp, tokrow = _compact(base, nwin, gpos.reshape(N_EXPERTS, nb, 1, TOK_BLOCK), payload, capacity)
    idx = tokrow[:, 1, :] * TOK_BLOCK + tokrow[:, 0, :]
    idx = jnp.clip(idx.astype(jnp.int32), 0, n - 1).reshape(-1)
    ye = _ffn(idx, xn, cmp, *expert_w, lw["layer"], capacity)

    expert_row0 = jnp.arange(N_EXPERTS, dtype=jnp.int32)[:, None] * capacity
    gpos2 = gpos.reshape(N_EXPERTS, n)
    grow = jnp.where(gpos2 >= 0, gpos2 + expert_row0.astype(F32) + 1.0, 0.0).T
    grow = jnp.pad(grow, ((0, 0), (0, LANES - N_EXPERTS)))
    row0 = expert_row0 + base
    first = (row0 // BF16_ROWS) * BF16_ROWS
    passes = jnp.where(cnt > 0, (row0 % BF16_ROWS + cnt + COMBINE_WIN - 1) // COMBINE_WIN, 1)
    return _combine(first.T, jnp.max(passes, axis=0), grow, ye, x1, final_g, final_norm)


def _trunk(x, layers, expert_w, final_g):
    b, s, d = x.shape
    h = x.reshape(b * s, d)
    for l, lw in enumerate(layers):
        h = _layer(h, lw, expert_w, final_g, seq_len=s, final_norm=(l == len(layers) - 1))
    return h.reshape(b, s, d)


def kernel(x_prompt, x_sample, norm1_g, w_in, pool_w, pool_scale, lru_conv_w, lru_conv_b, lru_wa, lru_ba, lru_wx, lru_bx, lru_lambda, conv_dw_w, conv_dw_b, conv_gn_g, conv_gn_b, conv_pw_w, conv_pw_b, w_out, norm2_g, router_w, exp_w_gate, exp_w_up, exp_w_down, final_g):
    p = dict(norm1_g=norm1_g, w_in=w_in, pool_w=pool_w, pool_scale=pool_scale, lru_conv_w=lru_conv_w,
             lru_conv_b=lru_conv_b, lru_wa=lru_wa, lru_ba=lru_ba, lru_wx=lru_wx, lru_bx=lru_bx,
             lru_lambda=lru_lambda, conv_dw_w=conv_dw_w, conv_dw_b=conv_dw_b, conv_gn_g=conv_gn_g,
             conv_gn_b=conv_gn_b, conv_pw_w=conv_pw_w, conv_pw_b=conv_pw_b, w_out=w_out,
             norm2_g=norm2_g, router_w=router_w, exp_w_gate=exp_w_gate, exp_w_up=exp_w_up,
             exp_w_down=exp_w_down)
    layers = [_prepare_layer(l, p) for l in range(norm1_g.shape[0])]
    fg = final_g.reshape(1, -1)
    expert_w = (exp_w_gate.astype(BF16), exp_w_up.astype(BF16), exp_w_down.astype(BF16))
    return (_trunk(x_prompt, layers, expert_w, fg), _trunk(x_sample, layers, expert_w, fg))
```

```python
import functools
import math

import jax
import jax.numpy as jnp
from jax import lax
from jax.experimental import pallas as pl
from jax.experimental.pallas import tpu as pltpu

F32 = jnp.float32
BF16 = jnp.bfloat16

D_MODEL = 1024
MIX_POOL = 256
MIX_LRU = 512
MIX_CONV = 256
IN_COLS = MIX_POOL + 2 * MIX_LRU + 2 * MIX_CONV
POOL_GROUP_WIDTH = 64
LRU_CONV_WIDTH = 4
RG_C = 8.0
CONV_WIDTH = 31
CONV_GROUPS = 4
N_EXPERTS = 16
CAPACITY_FACTOR = 2
D_FF = 11 * D_MODEL // 8
RMS_EPS = 1e-6
GN_EPS = 1e-5

SUBLANES = 8
ROW_TILE = 512
SCAN_TILE = 512
HALO = 16
TOK_BLOCK = 256
LANES = 128
BF16_ROWS = 16
COMPACT_WIN = 72
COMPACT_UNROLL = 8
FFN_ROWS = 1024
FFN_SUB = 512
COMBINE_WIN = 64
LANE_LOCAL = 64
VMEM_LIMIT = 48 * 1024 * 1024
FFN_VMEM_LIMIT = 56 * 1024 * 1024


def _cparams(*sem):
    return pltpu.CompilerParams(dimension_semantics=sem, vmem_limit_bytes=VMEM_LIMIT)


def _rms(x, g):
    return x * lax.rsqrt(jnp.mean(x * x, axis=-1, keepdims=True) + RMS_EPS) * g


def _split_bf16(x):
    hi = x.astype(BF16)
    lo = (x - hi.astype(F32)).astype(BF16)
    return hi, lo


def _dot(a, b):
    return jnp.dot(a, b, preferred_element_type=F32)


def _in_proj_kernel(x_ref, g_ref, w_ref, zp_ref, zl_ref, zg_ref, zc_ref):
    h = _rms(x_ref[...], g_ref[...])
    z = _dot(h.astype(BF16), w_ref[...])
    o1 = MIX_POOL
    o2 = o1 + MIX_LRU
    o3 = o2 + MIX_LRU
    zp_ref[...] = z[:, :o1]
    zl_ref[...] = z[:, o1:o2]
    zg_ref[...] = z[:, o2:o3]
    zc_ref[...] = z[:, o3:]


def _in_proj(x, g, w_bf16):
    n = x.shape[0]
    tm = min(ROW_TILE, n)
    row = lambda i: (i, 0)
    full = lambda i: (0, 0)
    widths = (MIX_POOL, MIX_LRU, MIX_LRU, 2 * MIX_CONV)
    return pl.pallas_call(
        _in_proj_kernel,
        grid=(n // tm,),
        in_specs=[pl.BlockSpec((tm, D_MODEL), row),
                  pl.BlockSpec((1, D_MODEL), full),
                  pl.BlockSpec((D_MODEL, IN_COLS), full)],
        out_specs=[pl.BlockSpec((tm, w), row) for w in widths],
        out_shape=[jax.ShapeDtypeStruct((n, w), F32) for w in widths],
        compiler_params=_cparams("parallel"),
        name="in_proj",
    )(x, g, w_bf16)


def _shift_rows(x, m):
    return x if m == 0 else pltpu.roll(x, (-m) % x.shape[0], axis=0)


def _lru_scan_kernel(*refs, tiles_per_seq, reverse):
    if reverse:
        c_ref, wa_ref, ba_ref, wx_ref, bx_ref, lam_ref, y_ref, a_ref, b_ref, h_ref = refs
    else:
        (u_ref, prev_ref, next_ref, cw_ref, cb_ref, wa_ref, ba_ref, wx_ref, bx_ref, lam_ref,
         y_ref, c_ref, a_ref, b_ref, h_ref) = refs
    t = y_ref.shape[0]
    j = pl.program_id(1)

    @pl.when(j == 0)
    def _():
        h_ref[...] = jnp.zeros_like(h_ref)

    if reverse:
        c = c_ref[...]
    else:
        ext = jnp.concatenate([jnp.where(j > 0, prev_ref[...], 0.0), u_ref[...],
                               jnp.where(j < tiles_per_seq - 1, next_ref[...], 0.0)], axis=0)
        pad_lo = LRU_CONV_WIDTH // 2
        c = cb_ref[...]
        for k in range(LRU_CONV_WIDTH):
            c = c + cw_ref[k:k + 1, :] * _shift_rows(ext, SUBLANES - pad_lo + k)[0:t, :]
        c_ref[...] = c
    cb16 = c.astype(BF16)
    r = jax.nn.sigmoid(_dot(cb16, wa_ref[...]) + ba_ref[...])
    i = jax.nn.sigmoid(_dot(cb16, wx_ref[...]) + bx_ref[...])
    neg_lam = -lam_ref[...]
    softplus = jnp.maximum(neg_lam, 0.0) + jnp.log1p(jnp.exp(-jnp.abs(neg_lam)))
    log_a = -RG_C * r * softplus
    th = jnp.tanh(log_a)
    one_minus_a2 = -2.0 * th / (1.0 - th)
    a_ref[...] = jnp.exp(log_a)
    b_ref[...] = jnp.sqrt(one_minus_a2) * (i * c)

    groups = t // SUBLANES
    sub = lax.broadcasted_iota(jnp.int32, (SUBLANES, MIX_LRU), 0)

    def body(g, carry):
        gi = groups - 1 - g if reverse else g
        start = pl.multiple_of(gi * SUBLANES, SUBLANES)
        a8 = a_ref[pl.ds(start, SUBLANES), :]
        b8 = b_ref[pl.ds(start, SUBLANES), :]
        for k in (1, 2, 4):
            shift = SUBLANES - k if reverse else k
            m = (sub < SUBLANES - k) if reverse else (sub >= k)
            a_sh = pltpu.roll(a8, shift, axis=0)
            b_sh = pltpu.roll(b8, shift, axis=0)
            b8 = jnp.where(m, a8 * b_sh + b8, b8)
            a8 = jnp.where(m, a8 * a_sh, a8)
        h = a8 * carry + b8
        y_ref[pl.ds(start, SUBLANES), :] = h
        edge = h[0:1, :] if reverse else h[SUBLANES - 1:SUBLANES, :]
        return jnp.broadcast_to(edge, (SUBLANES, MIX_LRU))

    h_ref[...] = lax.fori_loop(0, groups, body, h_ref[...], unroll=8)


def _lru_scan(src, cw, cb, wa_bd, ba, wx_bd, bx, lam, *, seq_len, reverse):
    n = src.shape[0]
    t = min(SCAN_TILE, seq_len)
    nt = seq_len // t
    nseq = n // seq_len
    halo_blocks = n // SUBLANES
    tile_blocks = t // SUBLANES

    def tile_of(s, j):
        return s * nt + ((nt - 1 - j) if reverse else j)

    def main_map(s, j):
        return (tile_of(s, j), 0)

    def prev_map(s, j):
        return (jnp.maximum(tile_of(s, j) * tile_blocks - 1, 0), 0)

    def next_map(s, j):
        return (jnp.minimum((tile_of(s, j) + 1) * tile_blocks, halo_blocks - 1), 0)

    full = lambda s, j: (0, 0)
    vec = pl.BlockSpec((1, MIX_LRU), full)
    mat = pl.BlockSpec((MIX_LRU, MIX_LRU), full)
    tile = pl.BlockSpec((t, MIX_LRU), main_map)
    out = jax.ShapeDtypeStruct((n, MIX_LRU), F32)
    gate_specs = [mat, vec, mat, vec, vec]
    if reverse:
        in_specs, args = [tile] + gate_specs, (src, wa_bd, ba, wx_bd, bx, lam)
        out_specs, out_shape = tile, out
    else:
        in_specs = [tile, pl.BlockSpec((SUBLANES, MIX_LRU), prev_map),
                    pl.BlockSpec((SUBLANES, MIX_LRU), next_map),
                    pl.BlockSpec((LRU_CONV_WIDTH, MIX_LRU), full), vec] + gate_specs
        args = (src, src, src, cw, cb, wa_bd, ba, wx_bd, bx, lam)
        out_specs, out_shape = [tile, tile], [out, out]
    return pl.pallas_call(
        functools.partial(_lru_scan_kernel, tiles_per_seq=nt, reverse=reverse),
        grid=(nseq, nt),
        in_specs=in_specs,
        out_specs=out_specs,
        out_shape=out_shape,
        scratch_shapes=[pltpu.VMEM((t, MIX_LRU), F32),
                        pltpu.VMEM((t, MIX_LRU), F32),
                        pltpu.VMEM((SUBLANES, MIX_LRU), F32)],
        compiler_params=_cparams("arbitrary", "arbitrary"),
        name="lru_scan_bwd" if reverse else "lru_scan_fwd",
    )(*args)


def _gelu_tanh(x):
    return 0.5 * x * (1.0 + jnp.tanh(0.7978845608028654 * (x + 0.044715 * (x * x * x))))


def _mix_out_kernel(x_ref, zp_ref, zp_prev_ref, zp_next_ref, zc_ref, zc_prev_ref, zc_next_ref,
                    zg_ref, yf_ref, yb_ref,
                    pool_w_ref, pool_s_ref, dw_w_ref, dw_b_ref, gn_g_ref, gn_b_ref, gavg_ref,
                    pw_w_ref, pw_b_ref, wout_ref, g2_ref, rw_hi_ref, rw_lo_ref,
                    x1_ref, xn_ref, aff_ref, payload_ref, *, tiles_per_seq, seq_len):
    t = x_ref.shape[0]
    jj = pl.program_id(0) % tiles_per_seq
    has_prev = jj > 0
    has_next = jj < tiles_per_seq - 1

    pext = jnp.concatenate([jnp.where(has_prev, zp_prev_ref[...], 0.0), zp_ref[...],
                            jnp.where(has_next, zp_next_ref[...], 0.0)], axis=0)
    lane = lax.broadcasted_iota(jnp.int32, (1, MIX_POOL), 1)
    half = jnp.left_shift(1, lane // POOL_GROUP_WIDTH)
    trailing = pext
    acc = None
    for g in range(MIX_POOL // POOL_GROUP_WIDTH):
        trailing = trailing + _shift_rows(trailing, -(1 << g))
        centred = _shift_rows(trailing, (1 << g) - 1)[HALO:HALO + t, :]
        acc = centred if acc is None else jnp.where(lane // POOL_GROUP_WIDTH >= g, centred, acc)
    pos = jj * t + lax.broadcasted_iota(jnp.int32, (t, MIX_POOL), 0)
    count = jnp.minimum(pos + half, seq_len) - jnp.maximum(pos - half, 0)
    dmean = acc / count.astype(F32) - zp_ref[...]
    y_pool = _dot(dmean.astype(BF16), pool_w_ref[...]) * pool_s_ref[...]

    def glu(z):
        return z[:, :MIX_CONV] * jax.nn.sigmoid(z[:, MIX_CONV:])

    vext = jnp.concatenate([glu(jnp.where(has_prev, zc_prev_ref[...], 0.0)), glu(zc_ref[...]),
                            glu(jnp.where(has_next, zc_next_ref[...], 0.0))], axis=0)
    v = jnp.zeros((t, MIX_CONV), F32) + dw_b_ref[...]
    first_off = HALO - CONV_WIDTH // 2
    for m in range(SUBLANES):
        shifted = _shift_rows(vext, m)
        for off in range(first_off + (m - first_off) % SUBLANES, first_off + CONV_WIDTH, SUBLANES):
            k = off - first_off
            v = v + dw_w_ref[k:k + 1, :] * shifted[off - m:off - m + t, :]
    gavg = gavg_ref[...]
    v_hi, v_lo = _split_bf16(v)
    mu = _dot(v_hi, gavg) + _dot(v_lo, gavg)
    dv = v - mu
    sq_hi, sq_lo = _split_bf16(dv * dv)
    var = _dot(sq_hi, gavg) + _dot(sq_lo, gavg)
    yn = dv * lax.rsqrt(var + GN_EPS) * gn_g_ref[...] + gn_b_ref[...]
    sw = yn * jax.nn.sigmoid(yn)
    y_conv = _dot(sw.astype(BF16), pw_w_ref[...]) + pw_b_ref[...]

    y_lru = (yf_ref[...] + yb_ref[...]) * _gelu_tanh(zg_ref[...])

    o1 = MIX_POOL
    o2 = o1 + MIX_LRU
    mixed = (_dot(y_pool.astype(BF16), wout_ref[0:o1, :])
             + _dot(y_lru.astype(BF16), wout_ref[o1:o2, :])
             + _dot(y_conv.astype(BF16), wout_ref[o2:, :]))
    x1 = x_ref[...] + mixed
    x1_ref[...] = x1

    xn = _rms(x1, g2_ref[...])
    xn_ref[...] = xn
    xn_hi, xn_lo = _split_bf16(xn)
    nt_dims = (((1,), (1,)), ((), ()))
    logits = (lax.dot_general(rw_hi_ref[...], xn_hi, nt_dims, preferred_element_type=F32)
              + lax.dot_general(rw_hi_ref[...], xn_lo, nt_dims, preferred_element_type=F32)
              + lax.dot_general(rw_lo_ref[...], xn_hi, nt_dims, preferred_element_type=F32))
    e = jnp.exp(logits - jnp.max(logits, axis=0, keepdims=True))
    aff = e / jnp.sum(e, axis=0, keepdims=True)
    aff_ref[...] = aff

    p0 = aff.astype(BF16).astype(F32)
    r1 = aff - p0
    p1 = r1.astype(BF16).astype(F32)
    p2 = r1 - p1
    tok = pl.program_id(0) * t + lax.broadcasted_iota(jnp.int32, (SUBLANES, t), 1)
    which = lax.broadcasted_iota(jnp.int32, (SUBLANES, t), 0)
    block = jnp.right_shift(tok, TOK_BLOCK.bit_length() - 1)
    meta = jnp.where(which == 0, tok & (TOK_BLOCK - 1), jnp.where(which == 1, block, 0))
    rows = [p0, p1, p2, jnp.zeros((LANE_LOCAL - 3 * N_EXPERTS, t), F32), meta.astype(F32),
            jnp.zeros((LANES - LANE_LOCAL - SUBLANES, t), F32)]
    payload_ref[...] = jnp.concatenate(rows, axis=0).T.astype(BF16)


def _mix_out(x, zp, zc, zg, yf, yb, wts, *, seq_len):
    n = x.shape[0]
    t = min(ROW_TILE, seq_len)
    nt = seq_len // t
    halo_blocks = n // HALO
    tile_blocks = t // HALO
    row = lambda i: (i, 0)
    prev = lambda i: (jnp.maximum(i * tile_blocks - 1, 0), 0)
    nxt = lambda i: (jnp.minimum((i + 1) * tile_blocks, halo_blocks - 1), 0)
    full = lambda i: (0, 0)

    def wspec(a):
        return pl.BlockSpec(a.shape, full)

    kern = functools.partial(_mix_out_kernel, tiles_per_seq=nt, seq_len=seq_len)
    return pl.pallas_call(
        kern,
        grid=(n // t,),
        in_specs=[pl.BlockSpec((t, D_MODEL), row),
                  pl.BlockSpec((t, MIX_POOL), row),
                  pl.BlockSpec((HALO, MIX_POOL), prev),
                  pl.BlockSpec((HALO, MIX_POOL), nxt),
                  pl.BlockSpec((t, 2 * MIX_CONV), row),
                  pl.BlockSpec((HALO, 2 * MIX_CONV), prev),
                  pl.BlockSpec((HALO, 2 * MIX_CONV), nxt),
                  pl.BlockSpec((t, MIX_LRU), row),
                  pl.BlockSpec((t, MIX_LRU), row),
                  pl.BlockSpec((t, MIX_LRU), row)] + [wspec(a) for a in wts],
        out_specs=[pl.BlockSpec((t, D_MODEL), row),
                   pl.BlockSpec((t, D_MODEL), row),
                   pl.BlockSpec((N_EXPERTS, t), lambda i: (0, i)),
                   pl.BlockSpec((t, LANES), row)],
        out_shape=[jax.ShapeDtypeStruct((n, D_MODEL), F32),
                   jax.ShapeDtypeStruct((n, D_MODEL), F32),
                   jax.ShapeDtypeStruct((N_EXPERTS, n), F32),
                   jax.ShapeDtypeStruct((n, LANES), BF16)],
        compiler_params=_cparams("parallel"),
        name="mix_out",
    )(x, zp, zp, zp, zc, zc, zc, zg, yf, yb, *wts)


def _select_kernel(aff_ref, gpos_ref, tot_ref, *, capacity, n_tokens):
    n_exp, nb = aff_ref.shape[0], aff_ref.shape[1]
    cap = jnp.float32(capacity)
    experts = range(n_exp)

    def count(mask):
        return jnp.sum(jnp.where(mask, 1.0, 0.0), keepdims=True)

    def as_f32(bits):
        return lax.bitcast_convert_type(bits, F32)

    def value_step(it, bits):
        bit = jnp.left_shift(1, 30 - it)
        out = []
        for e in experts:
            cand = bits[e] | bit
            out.append(jnp.where(count(aff_ref[e] >= as_f32(cand)) >= cap, cand, bits[e]))
        return tuple(out)

    zero = jnp.zeros((1, 1), jnp.int32)
    bits = lax.fori_loop(0, 31, value_step, (zero,) * n_exp)

    def masks(e):
        v = aff_ref[e]
        above = v >= as_f32(bits[e] + 1)
        return above, (v >= as_f32(bits[e])) & jnp.logical_not(above)

    need = [cap - count(masks(e)[0]) for e in experts]
    idx = (lax.broadcasted_iota(jnp.int32, (nb, TOK_BLOCK), 0) * TOK_BLOCK
           + lax.broadcasted_iota(jnp.int32, (nb, TOK_BLOCK), 1))
    index_bits = (n_tokens - 1).bit_length()

    def index_step(it, last):
        bit = jnp.left_shift(1, index_bits - 1 - it)
        out = []
        for e in experts:
            cand = last[e] | bit
            out.append(jnp.where(count(masks(e)[1] & (idx < cand)) < need[e], cand, last[e]))
        return tuple(out)

    last = lax.fori_loop(0, index_bits, index_step, (zero,) * n_exp)

    r = lax.broadcasted_iota(jnp.int32, (TOK_BLOCK, TOK_BLOCK), 0)
    c = lax.broadcasted_iota(jnp.int32, (TOK_BLOCK, TOK_BLOCK), 1)
    before = jnp.where(r < c, 1.0, 0.0).astype(BF16)
    ones = jnp.ones((TOK_BLOCK, TOK_BLOCK), BF16)
    rb = lax.broadcasted_iota(jnp.int32, (nb, nb), 0)
    cb = lax.broadcasted_iota(jnp.int32, (nb, nb), 1)
    earlier = jnp.where(cb < rb, 1.0, 0.0).astype(BF16)
    for e in experts:
        above, tie = masks(e)
        sel = above | (tie & (idx <= last[e]))
        sel16 = jnp.where(sel, 1.0, 0.0).astype(BF16)
        within = _dot(sel16, before)
        tot = _dot(sel16, ones)
        base = _dot(earlier, tot.astype(BF16))
        gpos_ref[e] = jnp.where(sel, base + within, -1.0)
        tot_ref[e] = tot[:, :LANES]


def _select(aff3, capacity):
    e, nb, _ = aff3.shape
    kern = functools.partial(_select_kernel, capacity=capacity, n_tokens=nb * TOK_BLOCK)
    whole = lambda i: (0, 0, 0)
    return pl.pallas_call(
        kern,
        grid=(1,),
        in_specs=[pl.BlockSpec((e, nb, TOK_BLOCK), whole)],
        out_specs=[pl.BlockSpec((e, nb, TOK_BLOCK), whole),
                   pl.BlockSpec((e, nb, LANES), whole)],
        out_shape=[jax.ShapeDtypeStruct((e, nb, TOK_BLOCK), F32),
                   jax.ShapeDtypeStruct((e, nb, LANES), F32)],
        compiler_params=_cparams("arbitrary"),
        name="select",
    )(aff3)


def _compact_kernel(base_ref, nwin_ref, gpos_ref, payload_ref, out_ref, tokrow_ref):
    e = pl.program_id(0)
    nb = gpos_ref.shape[0]
    out_ref[...] = jnp.zeros_like(out_ref)
    slot = lax.broadcasted_iota(jnp.int32, (COMPACT_WIN, TOK_BLOCK), 0).astype(F32)

    def moved(b, w):
        start = pl.multiple_of((base_ref[e * nb + b] // SUBLANES) * SUBLANES + w * COMPACT_WIN, SUBLANES)
        rows = payload_ref[pl.ds(pl.multiple_of(b * TOK_BLOCK, TOK_BLOCK), TOK_BLOCK), :]
        onehot = jnp.where(slot == gpos_ref[b] - start.astype(F32), 1.0, 0.0).astype(BF16)
        return start, _dot(onehot, rows)

    unroll = math.gcd(COMPACT_UNROLL, nb)

    def group(g, carry):
        blocks = [g * unroll + u for u in range(unroll)]
        for start, rows in [moved(b, 0) for b in blocks]:
            out_ref[pl.ds(start, COMPACT_WIN), :] += rows

        for b in blocks:
            def window(w, c, b=b):
                start, rows = moved(b, w)
                out_ref[pl.ds(start, COMPACT_WIN), :] += rows
                return c
            lax.fori_loop(1, nwin_ref[e * nb + b], window, 0)
        return carry

    lax.fori_loop(0, nb // unroll, group, 0)

    capacity = tokrow_ref.shape[1]
    pick = lax.broadcasted_iota(jnp.int32, (SUBLANES, LANES), 1) - LANE_LOCAL
    pick = jnp.where(pick == lax.broadcasted_iota(jnp.int32, (SUBLANES, LANES), 0), 1.0, 0.0)
    tokrow_ref[...] = lax.dot_general(pick.astype(BF16), out_ref[0:capacity, :].astype(BF16),
                                      (((1,), (1,)), ((), ())), preferred_element_type=F32)


def _compact(base, nwin, gpos4, payload, capacity):
    n_exp, nb = gpos4.shape[0], gpos4.shape[1]
    n = payload.shape[0]
    rows = capacity + COMPACT_WIN
    grid_spec = pltpu.PrefetchScalarGridSpec(
        num_scalar_prefetch=2,
        grid=(n_exp,),
        in_specs=[pl.BlockSpec((None, nb, 1, TOK_BLOCK), lambda e, b, w: (e, 0, 0, 0)),
                  pl.BlockSpec((n, LANES), lambda e, b, w: (0, 0))],
        out_specs=[pl.BlockSpec((None, rows, LANES), lambda e, b, w: (e, 0, 0)),
                   pl.BlockSpec((None, SUBLANES, capacity), lambda e, b, w: (e, 0, 0))],
    )
    return pl.pallas_call(
        _compact_kernel,
        grid_spec=grid_spec,
        out_shape=[jax.ShapeDtypeStruct((n_exp, rows, LANES), F32),
                   jax.ShapeDtypeStruct((n_exp, SUBLANES, capacity), F32)],
        compiler_params=_cparams("arbitrary"),
        name="compact",
    )(base.reshape(-1), nwin.reshape(-1), gpos4, payload)


def _ffn_kernel(idx_hbm, xn_hbm, cmp_ref, wg_ref, wu_ref, wd_ref, ye_ref,
                idx_smem, xbuf, idx_sem, row_sem, *, steps_per_expert, tf):
    e = pl.program_id(0)
    s = e * steps_per_expert + pl.program_id(1)
    total = pl.num_programs(0) * steps_per_expert
    step_rows = 2 * tf

    def idx_copy(step):
        half = step % 2
        return pltpu.make_async_copy(
            idx_hbm.at[pl.ds(pl.multiple_of(step * step_rows, step_rows), step_rows)],
            idx_smem.at[pl.ds(pl.multiple_of(half * step_rows, step_rows), step_rows)], idx_sem.at[half])

    def start_row(first_idx, buf, r):
        tok = idx_smem[first_idx + r]
        queue = r % 2 if isinstance(r, int) else 0
        pltpu.make_async_copy(xn_hbm.at[pl.ds(tok, 1)], xbuf.at[buf, pl.ds(r, 1)],
                              row_sem.at[buf]).start(priority=queue)

    def wait_rows(buf):
        pltpu.make_async_copy(xn_hbm.at[pl.ds(0, tf)], xbuf.at[buf], row_sem.at[buf]).wait()

    @pl.when(s == 0)
    def _():
        idx_copy(0).start()
        idx_copy(0).wait()

        def issue(r, c):
            start_row(0, 0, r)
            return c
        lax.fori_loop(0, tf, issue, 0, unroll=8)

    @pl.when(s + 1 < total)
    def _():
        idx_copy(s + 1).start()

    lane = lax.broadcasted_iota(jnp.int32, (step_rows, LANES), 1)
    gate_lanes = ((lane & (N_EXPERTS - 1)) == e) & (lane < 3 * N_EXPERTS)
    gate = jnp.sum(jnp.where(gate_lanes, cmp_ref[...], 0.0), axis=1, keepdims=True)
    sub = min(FFN_SUB, tf)

    def chunk(buf, out_row0, next_first_idx, next_buf):
        pieces = max(1, 3 * (tf // sub) - 2)
        issued = 0

        def issue_piece(p):
            nonlocal issued
            upto = min(tf, tf * (p + 1) // pieces)
            for r in range(issued, upto):
                start_row(next_first_idx, next_buf, r)
            issued = upto

        for hi, h0 in enumerate(range(0, tf, sub)):
            xe = xbuf[buf, h0:h0 + sub, :].astype(BF16)
            g = _dot(xe, wg_ref[...])
            issue_piece(3 * hi)
            u = _dot(xe, wu_ref[...])
            issue_piece(3 * hi + 1)
            h = (g * jax.nn.sigmoid(g)) * u
            r0 = out_row0 + h0
            ye_ref[r0:r0 + sub, :] = (_dot(h.astype(BF16), wd_ref[...]) * gate[r0:r0 + sub, :]).astype(BF16)
            issue_piece(3 * hi + 2)

    this_idx = (s % 2) * step_rows
    next_idx = ((s + 1) % 2) * step_rows
    wait_rows(0)
    chunk(0, 0, this_idx + tf, 1)

    @pl.when(s + 1 < total)
    def _():
        idx_copy(s + 1).wait()

    wait_rows(1)
    chunk(1, tf, next_idx, 0)

    @pl.when(s == total - 1)
    def _():
        wait_rows(0)


def _ffn(idx, xn, cmp, wg, wu, wd, layer, capacity):
    n_exp = wg.shape[1]
    tf = min(FFN_ROWS, capacity // 2)
    per = capacity // (2 * tf)
    kern = functools.partial(_ffn_kernel, steps_per_expert=per, tf=tf)
    weights = lambda e, i: (layer, e, 0, 0)
    return pl.pallas_call(
        kern,
        grid=(n_exp, per),
        in_specs=[pl.BlockSpec(memory_space=pl.ANY),
                  pl.BlockSpec(memory_space=pl.ANY),
                  pl.BlockSpec((None, 2 * tf, LANES), lambda e, i: (e, i, 0)),
                  pl.BlockSpec((None, None, D_MODEL, D_FF), weights),
                  pl.BlockSpec((None, None, D_MODEL, D_FF), weights),
                  pl.BlockSpec((None, None, D_FF, D_MODEL), weights)],
        out_specs=pl.BlockSpec((2 * tf, D_MODEL), lambda e, i: (e * per + i, 0)),
        out_shape=jax.ShapeDtypeStruct((n_exp * capacity, D_MODEL), BF16),
        scratch_shapes=[pltpu.SMEM((4 * tf,), jnp.int32),
                        pltpu.VMEM((2, tf, D_MODEL), F32),
                        pltpu.SemaphoreType.DMA((2,)),
                        pltpu.SemaphoreType.DMA((2,))],
        compiler_params=pltpu.CompilerParams(dimension_semantics=("arbitrary", "arbitrary"),
                                             vmem_limit_bytes=FFN_VMEM_LIMIT),
        name="ffn",
    )(idx, xn, cmp, wg, wu, wd)


def _combine_kernel(first_ref, npass_ref, grow_ref, spread_ref, ye_hbm, x1_ref, fg_ref, out_ref,
                    buf, buf_extra, sem, sem_extra, *, total_rows, final_norm):
    i = pl.program_id(0)
    nblocks = pl.num_programs(0)
    slot = i % 2
    last_start = total_rows - COMBINE_WIN

    def win_start(blk, e, p):
        return pl.multiple_of(jnp.minimum(first_ref[blk * N_EXPERTS + e] + p * COMBINE_WIN, last_start),
                              BF16_ROWS)

    def fetch(blk, p, dst, dsem):
        for e in range(N_EXPERTS):
            pltpu.make_async_copy(ye_hbm.at[pl.ds(win_start(blk, e, p), COMBINE_WIN)],
                                  dst.at[pl.ds(e * COMBINE_WIN, COMBINE_WIN)], dsem).start()

    def wait(dst, dsem):
        pltpu.make_async_copy(ye_hbm.at[pl.ds(0, N_EXPERTS * COMBINE_WIN)], dst, dsem).wait()

    @pl.when(i == 0)
    def _():
        fetch(0, 0, buf.at[0], sem.at[0])

    @pl.when(i + 1 < nblocks)
    def _():
        fetch(i + 1, 0, buf.at[1 - slot], sem.at[1 - slot])

    v = grow_ref[...]
    hi = jnp.floor(v * (1.0 / 256.0))
    lo = v - hi * 256.0
    spread = spread_ref[...]
    row = _dot(hi.astype(BF16), spread) * 256.0 + _dot(lo.astype(BF16), spread) - 1.0
    width = N_EXPERTS * COMBINE_WIN
    lane = lax.broadcasted_iota(jnp.int32, (1, width), 1)
    lane_blk = lane >> (COMBINE_WIN.bit_length() - 1)
    lane_in = (lane & (COMBINE_WIN - 1)).astype(F32)

    def onehot(p):
        start = jnp.zeros((1, width), F32)
        lower = jnp.zeros((1, width), F32)
        for e in range(N_EXPERTS):
            start = jnp.where(lane_blk == e, win_start(i, e, p).astype(F32), start)
            lower = jnp.where(lane_blk == e,
                              (first_ref[i * N_EXPERTS + e] + p * COMBINE_WIN).astype(F32), lower)
        rel = row - start
        if not isinstance(p, int) or p > 0:
            rel = jnp.where(row >= lower, rel, -1.0)
        return jnp.where(rel == lane_in, 1.0, 0.0).astype(BF16)

    wait(buf.at[slot], sem.at[slot])
    out_ref[...] = x1_ref[...] + _dot(onehot(0), buf[slot])

    def extra_pass(p, carry):
        fetch(i, p, buf_extra, sem_extra.at[0])
        wait(buf_extra, sem_extra.at[0])
        out_ref[...] += _dot(onehot(p), buf_extra[...])
        return carry

    lax.fori_loop(1, npass_ref[i], extra_pass, 0)

    if final_norm:
        out_ref[...] = _rms(out_ref[...], fg_ref[...])


def _combine(first, npass, grow, ye, x1, final_g, final_norm):
    n = x1.shape[0]
    width = N_EXPERTS * COMBINE_WIN
    lane = jnp.arange(width) // COMBINE_WIN
    spread = (jnp.arange(LANES)[:, None] == lane[None, :]).astype(BF16)
    tok = lambda i, f, p: (i, 0)
    full = lambda i, f, p: (0, 0)
    grid_spec = pltpu.PrefetchScalarGridSpec(
        num_scalar_prefetch=2,
        grid=(n // TOK_BLOCK,),
        in_specs=[pl.BlockSpec((TOK_BLOCK, LANES), tok),
                  pl.BlockSpec((LANES, width), full),
                  pl.BlockSpec(memory_space=pl.ANY),
                  pl.BlockSpec((TOK_BLOCK, D_MODEL), tok),
                  pl.BlockSpec((1, D_MODEL), full)],
        out_specs=pl.BlockSpec((TOK_BLOCK, D_MODEL), tok),
        scratch_shapes=[pltpu.VMEM((2, width, D_MODEL), BF16),
                        pltpu.VMEM((width, D_MODEL), BF16),
                        pltpu.SemaphoreType.DMA((2,)),
                        pltpu.SemaphoreType.DMA((1,))],
    )
    kern = functools.partial(_combine_kernel, total_rows=ye.shape[0], final_norm=final_norm)
    return pl.pallas_call(
        kern,
        grid_spec=grid_spec,
        out_shape=jax.ShapeDtypeStruct((n, D_MODEL), F32),
        compiler_params=_cparams("arbitrary"),
        name="combine",
    )(first.reshape(-1), npass, grow, spread, ye, x1, final_g)


def _block_diag(w):
    g, d, _ = w.shape
    eye = jnp.eye(g, dtype=w.dtype)
    return (eye[:, None, :, None] * w[:, :, None, :]).reshape(g * d, g * d)


def _group_mean_matrix(channels, groups):
    gid = jnp.arange(channels) // (channels // groups)
    return jnp.where(gid[:, None] == gid[None, :], groups / channels, 0.0).astype(BF16)


def _prepare_layer(l, p):
    row = lambda a: a.reshape(1, -1)
    rw_t = p["router_w"][l].T
    rw_hi = rw_t.astype(BF16)
    rw_lo = (rw_t - rw_hi.astype(F32)).astype(BF16)
    lru = []
    for d in range(2):
        lru.append((p["lru_conv_w"][l], row(p["lru_conv_b"][l]),
                    _block_diag(p["lru_wa"][l, d]).astype(BF16), row(p["lru_ba"][l, d]),
                    _block_diag(p["lru_wx"][l, d]).astype(BF16), row(p["lru_bx"][l, d]),
                    row(p["lru_lambda"][l, d])))
    mix = (_block_diag(p["pool_w"][l]).astype(BF16), row(p["pool_scale"][l]),
           p["conv_dw_w"][l], row(p["conv_dw_b"][l]), row(p["conv_gn_g"][l]), row(p["conv_gn_b"][l]),
           _group_mean_matrix(MIX_CONV, CONV_GROUPS),
           p["conv_pw_w"][l].astype(BF16), row(p["conv_pw_b"][l]),
           p["w_out"][l].astype(BF16), row(p["norm2_g"][l]), rw_hi, rw_lo)
    return dict(norm1_g=row(p["norm1_g"][l]), w_in=p["w_in"][l].astype(BF16), lru=lru, mix=mix, layer=l)


def _layer(x, lw, expert_w, final_g, *, seq_len, final_norm):
    n = x.shape[0]
    zp, zl, zg, zc = _in_proj(x, lw["norm1_g"], lw["w_in"])
    yf, conv = _lru_scan(zl, *lw["lru"][0], seq_len=seq_len, reverse=False)
    yb = _lru_scan(conv, *lw["lru"][1], seq_len=seq_len, reverse=True)
    x1, xn, aff, payload = _mix_out(x, zp, zc, zg, yf, yb, lw["mix"], seq_len=seq_len)

    capacity = max(1, CAPACITY_FACTOR * n // N_EXPERTS)
    nb = n // TOK_BLOCK
    gpos, tot = _select(aff.reshape(N_EXPERTS, nb, TOK_BLOCK), capacity)

    cnt = tot[:, :, 0].astype(jnp.int32)
    base = jnp.cumsum(cnt, axis=1) - cnt
    nwin = jnp.where(cnt > 0, (base % SUBLANES + cnt + COMPACT_WIN - 1) // COMPACT_WIN, 0)
    cmp, tokrow = _compact(base, nwin, gpos.reshape(N_EXPERTS, nb, 1, TOK_BLOCK), payload, capacity)
    idx = tokrow[:, 1, :] * TOK_BLOCK + tokrow[:, 0, :]
    idx = jnp.clip(idx.astype(jnp.int32), 0, n - 1).reshape(-1)
    ye = _ffn(idx, xn, cmp, *expert_w, lw["layer"], capacity)

    expert_row0 = jnp.arange(N_EXPERTS, dtype=jnp.int32)[:, None] * capacity
    gpos2 = gpos.reshape(N_EXPERTS, n)
    grow = jnp.where(gpos2 >= 0, gpos2 + expert_row0.astype(F32) + 1.0, 0.0).T
    grow = jnp.pad(grow, ((0, 0), (0, LANES - N_EXPERTS)))
    row0 = expert_row0 + base
    first = (row0 // BF16_ROWS) * BF16_ROWS
    passes = jnp.where(cnt > 0, (row0 % BF16_ROWS + cnt + COMBINE_WIN - 1) // COMBINE_WIN, 1)
    return _combine(first.T, jnp.max(passes, axis=0), grow, ye, x1, final_g, final_norm)


def _trunk(x, layers, expert_w, final_g):
    b, s, d = x.shape
    h = x.reshape(b * s, d)
    for l, lw in enumerate(layers):
        h = _layer(h, lw, expert_w, final_g, seq_len=s, final_norm=(l == len(layers) - 1))
    return h.reshape(b, s, d)


def kernel(x_prompt, x_sample, norm1_g, w_in, pool_w, pool_scale, lru_conv_w, lru_conv_b, lru_wa, lru_ba, lru_wx, lru_bx, lru_lambda, conv_dw_w, conv_dw_b, conv_gn_g, conv_gn_b, conv_pw_w, conv_pw_b, w_out, norm2_g, router_w, exp_w_gate, exp_w_up, exp_w_down, final_g):
    p = dict(norm1_g=norm1_g, w_in=w_in, pool_w=pool_w, pool_scale=pool_scale, lru_conv_w=lru_conv_w,
             lru_conv_b=lru_conv_b, lru_wa=lru_wa, lru_ba=lru_ba, lru_wx=lru_wx, lru_bx=lru_bx,
             lru_lambda=lru_lambda, conv_dw_w=conv_dw_w, conv_dw_b=conv_dw_b, conv_gn_g=conv_gn_g,
             conv_gn_b=conv_gn_b, conv_pw_w=conv_pw_w, conv_pw_b=conv_pw_b, w_out=w_out,
             norm2_g=norm2_g, router_w=router_w, exp_w_gate=exp_w_gate, exp_w_up=exp_w_up,
             exp_w_down=exp_w_down)
    layers = [_prepare_layer(l, p) for l in range(norm1_g.shape[0])]
    fg = final_g.reshape(1, -1)
    expert_w = (exp_w_gate.astype(BF16), exp_w_up.astype(BF16), exp_w_down.astype(BF16))
    return (_trunk(x_prompt, layers, expert_w, fg), _trunk(x_sample, layers, expert_w, fg))
```
